```python
import jax, jax.numpy as jnp
from jax import lax
import numpy as np

D_MODEL = 1024
BATCH = 2
SEQ = 8192
DEPTH = 4

D_PLE = 256
N_MIXERS = 2
GDN_HEADS = 8
GDN_DK = 128
GDN_DV = 128
GDN_CONV = 4
GDN_CHUNK = 64
GDN_QK = GDN_HEADS * GDN_DK
GDN_V = GDN_HEADS * GDN_DV
GDN_IN = 2 * GDN_QK + 2 * GDN_V + 2 * GDN_HEADS
FOX_HEADS = 16
FOX_DH = 64
FOX_BLOCK = 128
FOX_W = FOX_HEADS * FOX_DH
FOX_IN = 4 * FOX_W + FOX_HEADS
N_GDN = (DEPTH + 1) // 2
N_FOX = DEPTH // 2
DN_ALPHA = (2 * DEPTH) ** 0.25
DN_BETA = (8 * DEPTH) ** -0.25
LN_EPS = 1e-5
RMS_EPS = 1e-6

kernel_name = 'hybrid_gdn_fox_deepnorm_ple'

F32 = jnp.float32


def layer_norm(x, g, b):
    xf = x.astype(F32)
    mu = jnp.mean(xf, -1, keepdims=True)
    var = jnp.mean(jnp.square(xf - mu), -1, keepdims=True)
    return ((xf - mu) * lax.rsqrt(var + LN_EPS) * g + b).astype(x.dtype)


def rms_norm(x, g):
    xf = x.astype(F32)
    return (xf * lax.rsqrt(jnp.mean(xf * xf, -1, keepdims=True) + RMS_EPS) * g).astype(x.dtype)


def l2_normalize(x):
    xf = x.astype(F32)
    return xf * lax.rsqrt(jnp.sum(xf * xf, -1, keepdims=True) + RMS_EPS)


def causal_depthwise_conv(x, w):
    K, C = w.shape
    return lax.conv_general_dilated(x, w[:, None, :].astype(x.dtype), window_strides=(1,),
                                    padding=[(K - 1, 0)], dimension_numbers=('NWC', 'WIO', 'NWC'),
                                    feature_group_count=C)


def gated_delta_rule(q, k, v, beta, g):
    B, S, H, dk = q.shape
    dv = v.shape[-1]
    C = GDN_CHUNK
    N = S // C

    def to_chunks(t):
        return t.astype(F32).reshape(B, N, C, H, *t.shape[3:]).swapaxes(2, 3)

    q = to_chunks(q) * (dk ** -0.5)
    k = to_chunks(k)
    v = to_chunks(v)
    beta = to_chunks(beta)
    g = jnp.cumsum(to_chunks(g), axis=-1)
    causal = jnp.tril(jnp.ones((C, C), bool))
    strict = jnp.tril(jnp.ones((C, C), bool), -1)
    decay = jnp.exp(jnp.where(causal, g[..., :, None] - g[..., None, :], -jnp.inf))
    kb = k * beta[..., None]
    L = jnp.where(strict, jnp.einsum('bnhid,bnhjd->bnhij', kb, k) * decay, 0.0)
    eye = jnp.eye(C, dtype=F32)
    rhs = jnp.concatenate([v * beta[..., None], kb * jnp.exp(g)[..., None]], axis=-1)
    sol = lax.linalg.triangular_solve(eye + L, rhs, left_side=True, lower=True, unit_diagonal=True)
    u, w = sol[..., :dv], sol[..., dv:]
    a_qk = jnp.where(causal, jnp.einsum('bnhid,bnhjd->bnhij', q, k) * decay, 0.0)
    q_dec = q * jnp.exp(g)[..., None]
    k_dec = k * jnp.exp(g[..., -1:] - g)[..., None]
    g_last = jnp.exp(g[..., -1])

    def step(state, xs):
        q_c, k_c, u_c, w_c, a_c, gl = xs
        v_new = u_c - jnp.einsum('bhcd,bhde->bhce', w_c, state)
        o_c = jnp.einsum('bhcd,bhde->bhce', q_c, state) + jnp.einsum('bhij,bhje->bhie', a_c, v_new)
        state = state * gl[..., None, None] + jnp.einsum('bhcd,bhce->bhde', k_c, v_new)
        return state, o_c

    xs = tuple(jnp.moveaxis(t, 1, 0) for t in (q_dec, k_dec, u, w, a_qk, g_last))
    s0 = jnp.zeros((B, H, dk, dv), F32)
    _, o = lax.scan(step, s0, xs)
    return o.transpose(1, 0, 3, 2, 4).reshape(B, S, H, dv)


def gdn_mixer(x, w_in, conv_w, a_log, dt_bias, norm_g, w_out):
    B, S, _ = x.shape
    h = x @ w_in
    qkv, z, b_raw, a_raw = jnp.split(h, [2 * GDN_QK + GDN_V, 2 * GDN_QK + 2 * GDN_V,
                                         2 * GDN_QK + 2 * GDN_V + GDN_HEADS], axis=-1)
    qkv = jax.nn.silu(causal_depthwise_conv(qkv, conv_w))
    q, k, v = jnp.split(qkv, [GDN_QK, 2 * GDN_QK], axis=-1)
    q = l2_normalize(q.reshape(B, S, GDN_HEADS, GDN_DK))
    k = l2_normalize(k.reshape(B, S, GDN_HEADS, GDN_DK))
    v = v.reshape(B, S, GDN_HEADS, GDN_DV)
    beta = jax.nn.sigmoid(b_raw.astype(F32))
    g = -jnp.exp(a_log.astype(F32)) * jax.nn.softplus(a_raw.astype(F32) + dt_bias.astype(F32))
    o = gated_delta_rule(q, k, v, beta, g)
    o = rms_norm(o, norm_g) * jax.nn.silu(z.reshape(B, S, GDN_HEADS, GDN_DV).astype(F32))
    return o.reshape(B, S, GDN_V).astype(x.dtype) @ w_out


def forgetting_attention(q, k, v, c):
    B, S, H, Dh = q.shape
    nb = S // FOX_BLOCK
    kf = k.astype(F32)
    vf = v.astype(F32)
    c_t = c.transpose(0, 2, 1)
    q_blocks = (q.astype(F32) * (Dh ** -0.5)).reshape(B, nb, FOX_BLOCK, H, Dh).transpose(1, 0, 3, 2, 4)
    c_blocks = c_t.reshape(B, H, nb, FOX_BLOCK).transpose(2, 0, 1, 3)
    key_pos = jnp.arange(S)

    def one_block(args):
        qb, cb, blk = args
        logits = jnp.einsum('bhtd,bshd->bhts', qb, kf) + cb[..., None] - c_t[:, :, None, :]
        q_pos = blk * FOX_BLOCK + jnp.arange(FOX_BLOCK)
        logits = jnp.where(key_pos[None, :] <= q_pos[:, None], logits, -jnp.inf)
        probs = jax.nn.softmax(logits, axis=-1)
        return jnp.einsum('bhts,bshd->bthd', probs, vf)

    o = lax.map(one_block, (q_blocks, c_blocks, jnp.arange(nb)))
    return o.transpose(1, 0, 2, 3, 4).reshape(B, S, H, Dh)


def fox_mixer(x, w_in, b_f, q_norm_g, k_norm_g, w_out):
    B, S, _ = x.shape
    h = x @ w_in
    q, k, v, z, f_raw = jnp.split(h, [FOX_W, 2 * FOX_W, 3 * FOX_W, 4 * FOX_W], axis=-1)
    q = rms_norm(q.reshape(B, S, FOX_HEADS, FOX_DH), q_norm_g)
    k = rms_norm(k.reshape(B, S, FOX_HEADS, FOX_DH), k_norm_g)
    v = v.reshape(B, S, FOX_HEADS, FOX_DH)
    log_f = jax.nn.log_sigmoid(f_raw.astype(F32) + b_f.astype(F32))
    c = jnp.cumsum(log_f, axis=1)
    o = forgetting_attention(q, k, v, c)
    o = o * jax.nn.silu(z.reshape(B, S, FOX_HEADS, FOX_DH).astype(F32))
    return o.reshape(B, S, FOX_W).astype(x.dtype) @ w_out


def setup_inputs(seed: int = 0) -> dict:
    key = jax.random.key(seed)
    ks = jax.random.split(key, 20)
    nrm = jax.random.normal
    x = nrm(ks[0], (BATCH, SEQ, D_MODEL), F32)
    p = nrm(ks[1], (DEPTH, BATCH, SEQ, D_PLE), F32)
    ln_g = 1.0 + 0.02 * nrm(ks[2], (DEPTH, D_MODEL), F32)
    ln_b = 0.02 * nrm(ks[3], (DEPTH, D_MODEL), F32)
    ple_w_gate = nrm(ks[4], (DEPTH, D_MODEL, D_MODEL), F32) * D_MODEL ** -0.5
    ple_w_proj = nrm(ks[5], (DEPTH, D_PLE, D_MODEL), F32) * D_PLE ** -0.5
    gdn_w_in = nrm(ks[6], (N_GDN, D_MODEL, GDN_IN), F32) * D_MODEL ** -0.5
    gdn_conv_w = nrm(ks[7], (N_GDN, GDN_CONV, 2 * GDN_QK + GDN_V), F32) * GDN_CONV ** -0.5
    gdn_a_log = jnp.log(jax.random.uniform(ks[8], (N_GDN, GDN_HEADS), F32, 1.0, 16.0))
    dt = jnp.exp(jax.random.uniform(ks[9], (N_GDN, GDN_HEADS), F32, np.log(1e-3), np.log(1e-1)))
    gdn_dt_bias = dt + jnp.log(-jnp.expm1(-dt))
    gdn_norm_g = 1.0 + 0.02 * nrm(ks[10], (N_GDN, GDN_DV), F32)
    gdn_w_out = nrm(ks[11], (N_GDN, GDN_V, D_MODEL), F32) * (GDN_V ** -0.5) * DN_BETA
    fox_w_in = nrm(ks[12], (N_FOX, D_MODEL, FOX_IN), F32) * D_MODEL ** -0.5
    fox_b_f = jax.random.uniform(ks[13], (N_FOX, FOX_HEADS), F32, 1.0, 5.0)
    fox_q_norm_g = 1.0 + 0.02 * nrm(ks[14], (N_FOX, FOX_DH), F32)
    fox_k_norm_g = 1.0 + 0.02 * nrm(ks[15], (N_FOX, FOX_DH), F32)
    fox_w_out = nrm(ks[16], (N_FOX, FOX_W, D_MODEL), F32) * (FOX_W ** -0.5) * DN_BETA
    return {'x': x, 'p': p, 'ln_g': ln_g, 'ln_b': ln_b, 'ple_w_gate': ple_w_gate,
            'ple_w_proj': ple_w_proj, 'gdn_w_in': gdn_w_in, 'gdn_conv_w': gdn_conv_w,
            'gdn_a_log': gdn_a_log, 'gdn_dt_bias': gdn_dt_bias, 'gdn_norm_g': gdn_norm_g,
            'gdn_w_out': gdn_w_out, 'fox_w_in': fox_w_in, 'fox_b_f': fox_b_f,
            'fox_q_norm_g': fox_q_norm_g, 'fox_k_norm_g': fox_k_norm_g, 'fox_w_out': fox_w_out}


def reference(x, p, ln_g, ln_b, ple_w_gate, ple_w_proj, gdn_w_in, gdn_conv_w, gdn_a_log,
              gdn_dt_bias, gdn_norm_g, gdn_w_out, fox_w_in, fox_b_f, fox_q_norm_g,
              fox_k_norm_g, fox_w_out):
    for i in range(DEPTH):
        j = i // N_MIXERS
        if i % N_MIXERS == 0:
            y = gdn_mixer(x, gdn_w_in[j], gdn_conv_w[j], gdn_a_log[j], gdn_dt_bias[j],
                          gdn_norm_g[j], gdn_w_out[j])
        else:
            y = fox_mixer(x, fox_w_in[j], fox_b_f[j], fox_q_norm_g[j], fox_k_norm_g[j],
                          fox_w_out[j])
        x = layer_norm(DN_ALPHA * x + y, ln_g[i], ln_b[i])
        gate = jax.nn.sigmoid((x @ ple_w_gate[i]).astype(F32))
        x = x + (gate * (p[i] @ ple_w_proj[i]).astype(F32)).astype(x.dtype)
    return x
```

```python
import functools

import jax
import jax.numpy as jnp
from jax import lax
from jax.experimental import pallas as pl
from jax.experimental.pallas import tpu as pltpu

F32 = jnp.float32
BF16 = jnp.bfloat16
HIGHEST = lax.Precision.HIGHEST

LANES = 128
V7X_VMEM_BYTES = 64 * 1024 * 1024
VMEM_LIMIT = V7X_VMEM_BYTES * 3 // 4

GDN_CHUNK = 64
LN_EPS = 1e-5
RMS_EPS = 1e-6

_NT = (((1,), (1,)), ((), ()))
_TN = (((0,), (0,)), ((), ()))


def _mm(a, b):
    return jnp.dot(a.astype(BF16), b.astype(BF16), preferred_element_type=F32)


def _mm_f32(a, b):
    return jnp.dot(a, b, precision=HIGHEST, preferred_element_type=F32)


def _sigmoid(x):
    return 1.0 / (1.0 + jnp.exp(-x))


def _silu(x):
    return x * _sigmoid(x)


def _softplus(x):
    return jnp.maximum(x, 0.0) + jnp.log1p(jnp.exp(-jnp.abs(x)))


def _iota2(shape, dim):
    return lax.broadcasted_iota(jnp.int32, shape, dim)


def _params(*semantics):
    return pltpu.CompilerParams(dimension_semantics=semantics, vmem_limit_bytes=VMEM_LIMIT)


def _matmul_kernel(a_ref, w_ref, o_ref):
    o_ref[...] = _mm(a_ref[...], w_ref[...]).astype(o_ref.dtype)


def _matmul(a, w, *, tm, tn, out_dtype, name):
    m, k = a.shape
    n = w.shape[1]
    return pl.pallas_call(
        _matmul_kernel,
        grid=(m // tm, n // tn),
        in_specs=[pl.BlockSpec((tm, k), lambda i, j: (i, 0)),
                  pl.BlockSpec((k, tn), lambda i, j: (0, j))],
        out_specs=pl.BlockSpec((tm, tn), lambda i, j: (i, j)),
        out_shape=jax.ShapeDtypeStruct((m, n), out_dtype),
        compiler_params=_params("parallel", "parallel"),
        name=name,
    )(a, w)


def _gdn_gate_kernel(hs_ref, alog_ref, dtb_ref, beta_ref, gcum_ref, gcum_t_ref, *, rows):
    c = GDN_CHUNK
    beta_ref[...] = _sigmoid(hs_ref[:, :LANES])
    g = -jnp.exp(alog_ref[...]) * _softplus(hs_ref[:, LANES:] + dtb_ref[...])
    tril = (_iota2((c, c), 0) >= _iota2((c, c), 1)).astype(F32)
    eye = (_iota2((8, LANES), 0) == _iota2((8, LANES), 1)).astype(F32)
    for i in range(rows // c):
        gc = _mm_f32(tril, g[i * c:(i + 1) * c])
        gcum_ref[i * c:(i + 1) * c, :] = gc
        gcum_t_ref[i] = lax.dot_general(eye, gc, _NT, precision=HIGHEST, preferred_element_type=F32)


def _gdn_gates(hs, alog_row, dtb_row, *, rows):
    m = hs.shape[0]
    c = GDN_CHUNK
    return pl.pallas_call(
        functools.partial(_gdn_gate_kernel, rows=rows),
        grid=(m // rows,),
        in_specs=[pl.BlockSpec((rows, 2 * LANES), lambda i: (i, 0)),
                  pl.BlockSpec((1, LANES), lambda i: (0, 0)),
                  pl.BlockSpec((1, LANES), lambda i: (0, 0))],
        out_specs=[pl.BlockSpec((rows, LANES), lambda i: (i, 0)),
                   pl.BlockSpec((rows, LANES), lambda i: (i, 0)),
                   pl.BlockSpec((rows // c, 8, c), lambda i: (i, 0, 0))],
        out_shape=[jax.ShapeDtypeStruct((m, LANES), F32),
                   jax.ShapeDtypeStruct((m, LANES), F32),
                   jax.ShapeDtypeStruct((m // c, 8, c), F32)],
        compiler_params=_params("parallel"),
        name="gdn_gates",
    )(hs, alog_row, dtb_row)


def _gdn_core_kernel(hq_ref, hk_ref, hv_ref, hz_ref, wq_ref, wk_ref, wv_ref, beta_ref, gcum_ref,
                     gcum_t_ref, ng_ref, o_ref, state_ref, halo_ref, *, rows, dk):
    c = GDN_CHUNK
    head = pl.program_id(1)

    @pl.when(pl.program_id(2) == 0)
    def _():
        state_ref[...] = jnp.zeros_like(state_ref)
        halo_ref[...] = jnp.zeros_like(halo_ref)

    def conv_silu(x_ref, w_ref, slot):
        x = x_ref[...]
        xe = jnp.concatenate([halo_ref[slot], x], axis=0)
        w = w_ref[...]
        taps = w.shape[0]
        y = w[0:1, :] * xe[8 - taps + 1:8 - taps + 1 + rows]
        for t in range(1, taps):
            y = y + w[t:t + 1, :] * xe[8 - taps + 1 + t:8 - taps + 1 + t + rows]
        halo_ref[slot] = x[rows - 8:]
        return _silu(y)

    def l2norm(x):
        return x * lax.rsqrt(jnp.sum(x * x, axis=-1, keepdims=True) + RMS_EPS)

    q = l2norm(conv_silu(hq_ref, wq_ref, 0)) * (dk ** -0.5)
    k = l2norm(conv_silu(hk_ref, wk_ref, 1))
    v = conv_silu(hv_ref, wv_ref, 2)

    lane_is_head = _iota2((rows, LANES), 1) == head
    beta = jnp.sum(jnp.where(lane_is_head, beta_ref[...], 0.0), axis=1, keepdims=True)
    gcol_all = jnp.sum(jnp.where(lane_is_head, gcum_ref[...], 0.0), axis=1, keepdims=True)

    ri = _iota2((c, c), 0)
    ci = _iota2((c, c), 1)
    causal = ri >= ci
    strict = ri > ci
    ng = ng_ref[...]

    state = state_ref[...]
    for i in range(rows // c):
        sl = slice(i * c, (i + 1) * c)
        qc, kc, vc, bc, gcol = q[sl], k[sl], v[sl], beta[sl], gcol_all[sl]
        grow = gcum_t_ref[i, pl.ds(head, 1), :]
        glast = gcol[c - 1:c, :]
        eg = jnp.exp(gcol)
        decay = jnp.where(causal, jnp.exp(jnp.where(causal, gcol - grow, 0.0)), 0.0)
        kb = kc * bc
        gram = lax.dot_general(jnp.concatenate([kb, qc], axis=0).astype(BF16), kc.astype(BF16), _NT,
                               preferred_element_type=F32)
        lmat = jnp.where(strict, gram[:c] * decay, 0.0)
        amat = jnp.where(causal, gram[c:] * decay, 0.0)
        rhs = jnp.concatenate([vc * bc, kb * eg], axis=1)
        x = rhs - _mm_f32(lmat, rhs)
        lpow = lmat
        span = 2
        while span < c:
            lpow = _mm_f32(lpow, lpow)
            x = x + _mm_f32(lpow, x)
            span *= 2
        dv = vc.shape[1]
        ax = _mm(amat, x)
        kd = kc * jnp.exp(glast - gcol)
        kx = lax.dot_general(kd.astype(BF16), x.astype(BF16), _TN, preferred_element_type=F32)
        sb = state.astype(BF16)
        o = _mm(qc * eg - ax[:, dv:], sb) + ax[:, :dv]
        state = jnp.exp(glast) * state + kx[:, :dv] - _mm(kx[:, dv:], sb)
        o = o * lax.rsqrt(jnp.mean(o * o, axis=-1, keepdims=True) + RMS_EPS) * ng
        o_ref[sl, :] = (o * _silu(hz_ref[sl, :])).astype(o_ref.dtype)
    state_ref[...] = state


def _gdn_core(h_main, conv_w, beta, gcum, gcum_t, ng_row, *, batch, seq, heads, dk, rows):
    m = h_main.shape[0]
    nb = seq // rows
    c = GDN_CHUNK
    taps = conv_w.shape[0]

    def col(off):
        return pl.BlockSpec((rows, dk), lambda b, h, j: (b * nb + j, off + h))

    def wcol(off):
        return pl.BlockSpec((taps, dk), lambda b, h, j: (0, off + h))

    def gate():
        return pl.BlockSpec((rows, LANES), lambda b, h, j: (b * nb + j, 0))

    return pl.pallas_call(
        functools.partial(_gdn_core_kernel, rows=rows, dk=dk),
        grid=(batch, heads, nb),
        in_specs=[col(0), col(heads), col(2 * heads), col(3 * heads),
                  wcol(0), wcol(heads), wcol(2 * heads),
                  gate(), gate(),
                  pl.BlockSpec((rows // c, 8, c), lambda b, h, j: (b * nb + j, 0, 0)),
                  pl.BlockSpec((1, dk), lambda b, h, j: (0, 0))],
        out_specs=pl.BlockSpec((rows, dk), lambda b, h, j: (b * nb + j, h)),
        out_shape=jax.ShapeDtypeStruct((m, heads * dk), BF16),
        scratch_shapes=[pltpu.VMEM((dk, dk), F32), pltpu.VMEM((3, 8, dk), F32)],
        compiler_params=_params("parallel", "parallel", "arbitrary"),
        name="gdn_core",
    )(h_main, h_main, h_main, h_main, conv_w, conv_w, conv_w, beta, gcum, gcum_t, ng_row)


def _fox_prep_kernel(hq_ref, hk_ref, hv_ref, hf_ref, bf_ref, qg_ref, kg_ref,
                     qn_ref, kn_ref, vb_ref, c_ref, ct_ref, carry_ref, *, rows, heads, dh):
    @pl.when(pl.program_id(1) == 0)
    def _():
        carry_ref[...] = jnp.zeros_like(carry_ref)

    same_head = (_iota2((LANES, LANES), 0) // dh == _iota2((LANES, LANES), 1) // dh).astype(BF16)

    def head_rms(x_ref, g_ref, o_ref, scale):
        for j in range(x_ref.shape[1] // LANES):
            x = x_ref[:, j * LANES:(j + 1) * LANES]
            sq = x * x
            hi = sq.astype(BF16)
            lo = (sq - hi.astype(F32)).astype(BF16)
            ssum = (jnp.dot(hi, same_head, preferred_element_type=F32)
                    + jnp.dot(lo, same_head, preferred_element_type=F32))
            y = x * lax.rsqrt(ssum * (1.0 / dh) + RMS_EPS) * g_ref[:, j * LANES:(j + 1) * LANES]
            o_ref[:, j * LANES:(j + 1) * LANES] = (y * scale).astype(o_ref.dtype)

    head_rms(hq_ref, qg_ref, qn_ref, dh ** -0.5)
    head_rms(hk_ref, kg_ref, kn_ref, 1.0)
    vb_ref[...] = hv_ref[...].astype(vb_ref.dtype)

    logf = -_softplus(-(hf_ref[...] + bf_ref[...]))
    logf = jnp.where(_iota2((rows, LANES), 1) < heads, logf, 0.0)
    tril = (_iota2((rows, rows), 0) >= _iota2((rows, rows), 1)).astype(F32)
    c = _mm_f32(tril, logf) + carry_ref[...]
    carry_ref[...] = c[rows - 1:rows, :]
    c_ref[...] = c
    eye = (_iota2((heads, LANES), 0) == _iota2((heads, LANES), 1)).astype(F32)
    ct_ref[0] = lax.dot_general(eye, c, _NT, precision=HIGHEST, preferred_element_type=F32)


def _fox_prep(h_main, hf, bf_row, qg_row, kg_row, *, batch, seq, heads, dh, rows):
    m = h_main.shape[0]
    w = heads * dh
    nt = seq // rows

    def col(j):
        return pl.BlockSpec((rows, w), lambda b, t: (b * nt + t, j))

    def row(n):
        return pl.BlockSpec((1, n), lambda b, t: (0, 0))

    return pl.pallas_call(
        functools.partial(_fox_prep_kernel, rows=rows, heads=heads, dh=dh),
        grid=(batch, nt),
        in_specs=[col(0), col(1), col(2),
                  pl.BlockSpec((rows, LANES), lambda b, t: (b * nt + t, 0)),
                  row(LANES), row(w), row(w)],
        out_specs=[col(0), col(0), col(0),
                   pl.BlockSpec((rows, LANES), lambda b, t: (b * nt + t, 0)),
                   pl.BlockSpec((1, heads, rows), lambda b, t: (b, 0, t))],
        out_shape=[jax.ShapeDtypeStruct((m, w), BF16), jax.ShapeDtypeStruct((m, w), BF16),
                   jax.ShapeDtypeStruct((m, w), BF16), jax.ShapeDtypeStruct((m, LANES), F32),
                   jax.ShapeDtypeStruct((batch, heads, seq), F32)],
        scratch_shapes=[pltpu.VMEM((1, LANES), F32)],
        compiler_params=_params("parallel", "arbitrary"),
        name="fox_prep",
    )(h_main, h_main, h_main, hf, bf_row, qg_row, kg_row)


def _fox_attn_kernel(q_ref, k_ref, v_ref, c_ref, ct_ref, z_ref, o_ref, *, tq, dh):
    qi = pl.program_id(2)
    pair = pl.program_id(1)
    heads_per_block = LANES // dh
    lane = _iota2((tq, LANES), 1)
    causal = _iota2((tq, tq), 0) >= _iota2((tq, tq), 1)
    c_blk = c_ref[...]
    outs = []
    for a in range(heads_per_block):
        head = pair * heads_per_block + a
        lanes = slice(a * dh, (a + 1) * dh)
        qa = q_ref[:, lanes]
        ccol = jnp.sum(jnp.where(lane == head, c_blk, 0.0), axis=1, keepdims=True)

        def tile(kj, carry, masked, qa=qa, ccol=ccol, head=head, lanes=lanes):
            m_prev, l_prev, acc = carry
            start = pl.multiple_of(kj * tq, tq)
            kt = k_ref[pl.ds(start, tq), lanes]
            vt = v_ref[pl.ds(start, tq), lanes]
            crow = ct_ref[0, pl.ds(head, 1), pl.ds(start, tq)]
            s = lax.dot_general(qa, kt, _NT, preferred_element_type=F32) + (ccol - crow)
            if masked:
                s = jnp.where(causal, s, -jnp.inf)
            m_new = jnp.maximum(m_prev, jnp.max(s, axis=1, keepdims=True))
            alpha = jnp.exp(m_prev - m_new)
            p = jnp.exp(s - m_new)
            l_new = alpha * l_prev + jnp.sum(p, axis=1, keepdims=True)
            acc = alpha * acc + jnp.dot(p.astype(BF16), vt, preferred_element_type=F32)
            return m_new, l_new, acc

        init = (jnp.full((tq, 1), -1e30, F32), jnp.zeros((tq, 1), F32), jnp.zeros((tq, dh), F32))
        carry = lax.fori_loop(0, qi, functools.partial(tile, masked=False), init)
        _, l_fin, acc = tile(qi, carry, True)
        outs.append(acc / l_fin)
    o = jnp.concatenate(outs, axis=1)
    o_ref[...] = (o * _silu(z_ref[...])).astype(o_ref.dtype)


def _fox_attn(qn, kn, vb, c, ct, h_main, *, batch, seq, heads, dh, tq):
    m = qn.shape[0]
    nq = seq // tq
    pairs = heads * dh // LANES
    z_off = 3 * pairs
    return pl.pallas_call(
        functools.partial(_fox_attn_kernel, tq=tq, dh=dh),
        grid=(batch, pairs, nq),
        in_specs=[pl.BlockSpec((tq, LANES), lambda b, h, i: (b * nq + i, h)),
                  pl.BlockSpec((seq, LANES), lambda b, h, i: (b, h)),
                  pl.BlockSpec((seq, LANES), lambda b, h, i: (b, h)),
                  pl.BlockSpec((tq, LANES), lambda b, h, i: (b * nq + i, 0)),
                  pl.BlockSpec((1, heads, seq), lambda b, h, i: (b, 0, 0)),
                  pl.BlockSpec((tq, LANES), lambda b, h, i: (b * nq + i, z_off + h))],
        out_specs=pl.BlockSpec((tq, LANES), lambda b, h, i: (b * nq + i, h)),
        out_shape=jax.ShapeDtypeStruct((m, heads * dh), BF16),
        compiler_params=_params("parallel", "parallel", "arbitrary"),
        name="fox_attn",
    )(qn, kn, vb, c, ct, h_main)


def _post_kernel(a_ref, x_ref, p_ref, wo_ref, g_ref, b_ref, wg_ref, wp_ref, o_ref, *, alpha):
    t = alpha * x_ref[...] + jnp.dot(a_ref[...], wo_ref[...], preferred_element_type=F32)
    mu = jnp.mean(t, axis=-1, keepdims=True)
    d = t - mu
    var = jnp.mean(d * d, axis=-1, keepdims=True)
    xn = d * lax.rsqrt(var + LN_EPS) * g_ref[...] + b_ref[...]
    gate = _sigmoid(_mm(xn, wg_ref[...]))
    o_ref[...] = xn + gate * _mm(p_ref[...], wp_ref[...])


def _post(a, x, p, w_out, ln_g, ln_b, w_gate, w_proj, *, alpha, tm):
    m, d = x.shape
    dp = p.shape[1]

    def rows(n):
        return pl.BlockSpec((tm, n), lambda i: (i, 0))

    def full(r, n):
        return pl.BlockSpec((r, n), lambda i: (0, 0))

    return pl.pallas_call(
        functools.partial(_post_kernel, alpha=alpha),
        grid=(m // tm,),
        in_specs=[rows(a.shape[1]), rows(d), rows(dp), full(a.shape[1], d), full(1, d), full(1, d),
                  full(d, d), full(dp, d)],
        out_specs=rows(d),
        out_shape=jax.ShapeDtypeStruct((m, d), F32),
        compiler_params=_params("parallel"),
        name="post_block",
    )(a, x, p, w_out, ln_g, ln_b, w_gate, w_proj)


def _pad_cols(w, n):
    return jnp.pad(w, ((0, 0), (0, n - w.shape[1])))


def _pad_row(v, n):
    return jnp.pad(v, (0, n - v.shape[0])).reshape(1, n)


def _gdn_layer(x, w_in, conv_w, a_log, dt_bias, norm_g, *, batch, seq):
    heads = a_log.shape[0]
    dk = norm_g.shape[0]
    wide = 4 * heads * dk
    h_main = _matmul(x, w_in[:, :wide].astype(BF16), tm=512, tn=1024, out_dtype=F32, name="gdn_in_proj")
    w_gates = jnp.concatenate([_pad_cols(w_in[:, wide:wide + heads], LANES),
                               _pad_cols(w_in[:, wide + heads:], LANES)], axis=1).astype(BF16)
    hs = _matmul(x, w_gates, tm=512, tn=2 * LANES, out_dtype=F32, name="gdn_gate_proj")
    beta, gcum, gcum_t = _gdn_gates(hs, _pad_row(a_log, LANES), _pad_row(dt_bias, LANES), rows=512)
    return _gdn_core(h_main, conv_w, beta, gcum, gcum_t, norm_g.reshape(1, dk),
                     batch=batch, seq=seq, heads=heads, dk=dk, rows=512)


def _fox_layer(x, w_in, b_f, q_norm_g, k_norm_g, *, batch, seq):
    heads = b_f.shape[0]
    dh = q_norm_g.shape[0]
    wide = 4 * heads * dh
    h_main = _matmul(x, w_in[:, :wide].astype(BF16), tm=512, tn=1024, out_dtype=F32, name="fox_in_proj")
    hf = _matmul(x, _pad_cols(w_in[:, wide:], LANES).astype(BF16), tm=512, tn=LANES, out_dtype=F32,
                 name="fox_forget_proj")
    qn, kn, vb, c, ct = _fox_prep(h_main, hf, _pad_row(b_f, LANES),
                                  jnp.tile(q_norm_g, heads).reshape(1, heads * dh),
                                  jnp.tile(k_norm_g, heads).reshape(1, heads * dh),
                                  batch=batch, seq=seq, heads=heads, dh=dh, rows=512)
    return _fox_attn(qn, kn, vb, c, ct, h_main, batch=batch, seq=seq, heads=heads, dh=dh, tq=256)


def kernel(x, p, ln_g, ln_b, ple_w_gate, ple_w_proj, gdn_w_in, gdn_conv_w, gdn_a_log, gdn_dt_bias,
           gdn_norm_g, gdn_w_out, fox_w_in, fox_b_f, fox_q_norm_g, fox_k_norm_g, fox_w_out):
    batch, seq, d = x.shape
    depth = ln_g.shape[0]
    alpha = (2 * depth) ** 0.25
    xf = x.reshape(batch * seq, d)
    for i in range(depth):
        j = i // 2
        if i % 2 == 0:
            a = _gdn_layer(xf, gdn_w_in[j], gdn_conv_w[j], gdn_a_log[j], gdn_dt_bias[j], gdn_norm_g[j],
                           batch=batch, seq=seq)
            w_out = gdn_w_out[j]
        else:
            a = _fox_layer(xf, fox_w_in[j], fox_b_f[j], fox_q_norm_g[j], fox_k_norm_g[j],
                           batch=batch, seq=seq)
            w_out = fox_w_out[j]
        xf = _post(a, xf, p[i].reshape(batch * seq, -1), w_out.astype(BF16), ln_g[i].reshape(1, d),
                   ln_b[i].reshape(1, d), ple_w_gate[i].astype(BF16), ple_w_proj[i].astype(BF16),
                   alpha=alpha, tm=512)
    return xf.reshape(batch, seq, d)
```

```python
import functools

import jax
import jax.numpy as jnp
from jax import lax
from jax.experimental import pallas as pl
from jax.experimental.pallas import tpu as pltpu

F32 = jnp.float32
BF16 = jnp.bfloat16
HIGHEST = lax.Precision.HIGHEST

LANES = 128
V7X_VMEM_BYTES = 64 * 1024 * 1024
VMEM_LIMIT = V7X_VMEM_BYTES * 3 // 4

GDN_CHUNK = 64
LN_EPS = 1e-5
RMS_EPS = 1e-6
LOG2E = 1.4426950408889634

_NT = (((1,), (1,)), ((), ()))
_TN = (((0,), (0,)), ((), ()))


def _mm(a, b):
    return jnp.dot(a.astype(BF16), b.astype(BF16), preferred_element_type=F32)


def _mm_f32(a, b):
    return jnp.dot(a, b, precision=HIGHEST, preferred_element_type=F32)


def _sigmoid(x):
    return 1.0 / (1.0 + jnp.exp(-x))


def _silu(x):
    return x * _sigmoid(x)


def _softplus(x):
    return jnp.maximum(x, 0.0) + jnp.log1p(jnp.exp(-jnp.abs(x)))


def _iota2(shape, dim):
    return lax.broadcasted_iota(jnp.int32, shape, dim)


def _params(*semantics):
    return pltpu.CompilerParams(dimension_semantics=semantics, vmem_limit_bytes=VMEM_LIMIT)


def _matmul_kernel(a_ref, w_ref, o_ref):
    o_ref[...] = _mm(a_ref[...], w_ref[...]).astype(o_ref.dtype)


def _matmul(a, w, *, tm, tn, out_dtype, name):
    m, k = a.shape
    n = w.shape[1]
    return pl.pallas_call(
        _matmul_kernel,
        grid=(m // tm, n // tn),
        in_specs=[pl.BlockSpec((tm, k), lambda i, j: (i, 0)),
                  pl.BlockSpec((k, tn), lambda i, j: (0, j))],
        out_specs=pl.BlockSpec((tm, tn), lambda i, j: (i, j)),
        out_shape=jax.ShapeDtypeStruct((m, n), out_dtype),
        compiler_params=_params("parallel", "parallel"),
        name=name,
    )(a, w)


def _gdn_gate_kernel(hs_ref, alog_ref, dtb_ref, beta_ref, gcum_ref, gcum_t_ref, *, rows):
    c = GDN_CHUNK
    beta_ref[...] = _sigmoid(hs_ref[:, :LANES])
    g = -jnp.exp(alog_ref[...]) * _softplus(hs_ref[:, LANES:] + dtb_ref[...])
    tril = (_iota2((c, c), 0) >= _iota2((c, c), 1)).astype(F32)
    eye = (_iota2((8, LANES), 0) == _iota2((8, LANES), 1)).astype(F32)
    for i in range(rows // c):
        gc = _mm_f32(tril, g[i * c:(i + 1) * c])
        gcum_ref[i * c:(i + 1) * c, :] = gc
        gcum_t_ref[i] = lax.dot_general(eye, gc, _NT, precision=HIGHEST, preferred_element_type=F32)


def _gdn_gates(hs, alog_row, dtb_row, *, rows):
    m = hs.shape[0]
    c = GDN_CHUNK
    return pl.pallas_call(
        functools.partial(_gdn_gate_kernel, rows=rows),
        grid=(m // rows,),
        in_specs=[pl.BlockSpec((rows, 2 * LANES), lambda i: (i, 0)),
                  pl.BlockSpec((1, LANES), lambda i: (0, 0)),
                  pl.BlockSpec((1, LANES), lambda i: (0, 0))],
        out_specs=[pl.BlockSpec((rows, LANES), lambda i: (i, 0)),
                   pl.BlockSpec((rows, LANES), lambda i: (i, 0)),
                   pl.BlockSpec((rows // c, 8, c), lambda i: (i, 0, 0))],
        out_shape=[jax.ShapeDtypeStruct((m, LANES), F32),
                   jax.ShapeDtypeStruct((m, LANES), F32),
                   jax.ShapeDtypeStruct((m // c, 8, c), F32)],
        compiler_params=_params("parallel"),
        name="gdn_gates",
    )(hs, alog_row, dtb_row)


def _gdn_core_kernel(hq_ref, hk_ref, hv_ref, hz_ref, wq_ref, wk_ref, wv_ref, beta_ref, gcum_ref,
                     gcum_t_ref, ng_ref, o_ref, state_ref, halo_ref, *, rows, dk):
    c = GDN_CHUNK
    head = pl.program_id(1)

    @pl.when(pl.program_id(2) == 0)
    def _():
        state_ref[...] = jnp.zeros_like(state_ref)
        halo_ref[...] = jnp.zeros_like(halo_ref)

    def conv_silu(x_ref, w_ref, slot):
        x = x_ref[...]
        xe = jnp.concatenate([halo_ref[slot], x], axis=0)
        w = w_ref[...]
        taps = w.shape[0]
        y = w[0:1, :] * xe[8 - taps + 1:8 - taps + 1 + rows]
        for t in range(1, taps):
            y = y + w[t:t + 1, :] * xe[8 - taps + 1 + t:8 - taps + 1 + t + rows]
        halo_ref[slot] = x[rows - 8:]
        return _silu(y)

    def l2norm(x):
        return x * lax.rsqrt(jnp.sum(x * x, axis=-1, keepdims=True) + RMS_EPS)

    q = l2norm(conv_silu(hq_ref, wq_ref, 0)) * (dk ** -0.5)
    k = l2norm(conv_silu(hk_ref, wk_ref, 1))
    v = conv_silu(hv_ref, wv_ref, 2)

    lane_is_head = _iota2((rows, LANES), 1) == head
    beta = jnp.sum(jnp.where(lane_is_head, beta_ref[...], 0.0), axis=1, keepdims=True)
    gcol_all = jnp.sum(jnp.where(lane_is_head, gcum_ref[...], 0.0), axis=1, keepdims=True)

    ri = _iota2((c, c), 0)
    ci = _iota2((c, c), 1)
    causal = ri >= ci
    strict = ri > ci
    ng = ng_ref[...]

    state = state_ref[...]
    for i in range(rows // c):
        sl = slice(i * c, (i + 1) * c)
        qc, kc, vc, bc, gcol = q[sl], k[sl], v[sl], beta[sl], gcol_all[sl]
        grow = gcum_t_ref[i, pl.ds(head, 1), :]
        glast = gcol[c - 1:c, :]
        eg = jnp.exp(gcol)
        decay = jnp.where(causal, jnp.exp(jnp.where(causal, gcol - grow, 0.0)), 0.0)
        kb = kc * bc
        gram = lax.dot_general(jnp.concatenate([kb, qc], axis=0).astype(BF16), kc.astype(BF16), _NT,
                               preferred_element_type=F32)
        lmat = jnp.where(strict, gram[:c] * decay, 0.0)
        amat = jnp.where(causal, gram[c:] * decay, 0.0)
        rhs = jnp.concatenate([vc * bc, kb * eg], axis=1)
        x = rhs - _mm_f32(lmat, rhs)
        lpow = lmat
        span = 2
        while span < c:
            lpow = _mm_f32(lpow, lpow)
            x = x + _mm_f32(lpow, x)
            span *= 2
        dv = vc.shape[1]
        ax = _mm(amat, x)
        kd = kc * jnp.exp(glast - gcol)
        kx = lax.dot_general(kd.astype(BF16), x.astype(BF16), _TN, preferred_element_type=F32)
        sb = state.astype(BF16)
        o = _mm(qc * eg - ax[:, dv:], sb) + ax[:, :dv]
        state = jnp.exp(glast) * state + kx[:, :dv] - _mm(kx[:, dv:], sb)
        o = o * lax.rsqrt(jnp.mean(o * o, axis=-1, keepdims=True) + RMS_EPS) * ng
        o_ref[sl, :] = (o * _silu(hz_ref[sl, :])).astype(o_ref.dtype)
    state_ref[...] = state


def _gdn_core(h_main, conv_w, beta, gcum, gcum_t, ng_row, *, batch, seq, heads, dk, rows):
    m = h_main.shape[0]
    nb = seq // rows
    c = GDN_CHUNK
    taps = conv_w.shape[0]

    def col(off):
        return pl.BlockSpec((rows, dk), lambda b, h, j: (b * nb + j, off + h))

    def wcol(off):
        return pl.BlockSpec((taps, dk), lambda b, h, j: (0, off + h))

    def gate():
        return pl.BlockSpec((rows, LANES), lambda b, h, j: (b * nb + j, 0))

    return pl.pallas_call(
        functools.partial(_gdn_core_kernel, rows=rows, dk=dk),
        grid=(batch, heads, nb),
        in_specs=[col(0), col(heads), col(2 * heads), col(3 * heads),
                  wcol(0), wcol(heads), wcol(2 * heads),
                  gate(), gate(),
                  pl.BlockSpec((rows // c, 8, c), lambda b, h, j: (b * nb + j, 0, 0)),
                  pl.BlockSpec((1, dk), lambda b, h, j: (0, 0))],
        out_specs=pl.BlockSpec((rows, dk), lambda b, h, j: (b * nb + j, h)),
        out_shape=jax.ShapeDtypeStruct((m, heads * dk), BF16),
        scratch_shapes=[pltpu.VMEM((dk, dk), F32), pltpu.VMEM((3, 8, dk), F32)],
        compiler_params=_params("parallel", "parallel", "arbitrary"),
        name="gdn_core",
    )(h_main, h_main, h_main, h_main, conv_w, conv_w, conv_w, beta, gcum, gcum_t, ng_row)


FOX_VROWS = 80


def _split3(x):
    hi = x.astype(BF16).astype(F32)
    r = x - hi
    mid = r.astype(BF16).astype(F32)
    lo = (r - mid).astype(BF16).astype(F32)
    return hi, mid, lo


def _fox_prep_kernel(hq_ref, hk_ref, hv_ref, hf_ref, bf_ref, qg_ref, kg_ref,
                     qa_ref, ka_ref, vt_ref, carry_ref, *, rows, heads, dh):
    @pl.when(pl.program_id(1) == 0)
    def _():
        carry_ref[...] = jnp.zeros_like(carry_ref)

    logf = -_softplus(-(hf_ref[...] + bf_ref[...]))
    lane = _iota2((rows, LANES), 1)
    logf = jnp.where(lane < heads, logf, 0.0)
    tril = (_iota2((rows, rows), 0) >= _iota2((rows, rows), 1)).astype(F32)
    c = _mm_f32(tril, logf) + carry_ref[...]
    carry_ref[...] = c[rows - 1:rows, :]
    c2 = c * LOG2E

    same_head = (_iota2((LANES, LANES), 0) // dh == _iota2((LANES, LANES), 1) // dh).astype(BF16)

    def head_rms(x, g, scale):
        sq = x * x
        hi = sq.astype(BF16)
        lo = (sq - hi.astype(F32)).astype(BF16)
        ssum = (jnp.dot(hi, same_head, preferred_element_type=F32)
                + jnp.dot(lo, same_head, preferred_element_type=F32))
        return x * lax.rsqrt(ssum * (1.0 / dh) + RMS_EPS) * (g * scale)

    per_block = LANES // dh
    bias_lane = dh
    vrow = _iota2((FOX_VROWS, LANES), 0)
    vlane = _iota2((FOX_VROWS, LANES), 1)
    out_row = _iota2((FOX_VROWS, rows), 0)
    for j in range(heads // per_block):
        cols = slice(j * LANES, (j + 1) * LANES)
        qn = head_rms(hq_ref[:, cols], qg_ref[:, cols], dh ** -0.5 * LOG2E)
        kn = head_rms(hk_ref[:, cols], kg_ref[:, cols], 1.0)
        vb = hv_ref[:, cols].astype(BF16)
        for a in range(per_block):
            h = j * per_block + a
            chi, cmid, clo = _split3(jnp.sum(jnp.where(lane == h, c2, 0.0), axis=1, keepdims=True))
            qh = qn if a == 0 else pltpu.roll(qn, LANES - a * dh, axis=1)
            kh = kn if a == 0 else pltpu.roll(kn, LANES - a * dh, axis=1)
            q_tail = jnp.where(lane == bias_lane, chi, jnp.where(lane == bias_lane + 1, cmid, jnp.where(
                lane == bias_lane + 2, clo, jnp.where(lane < bias_lane + 6, 1.0, 0.0))))
            k_tail = jnp.where(lane < bias_lane + 3, 1.0, jnp.where(lane == bias_lane + 3, -chi, jnp.where(
                lane == bias_lane + 4, -cmid, jnp.where(lane == bias_lane + 5, -clo, 0.0))))
            out_cols = slice(h * LANES, (h + 1) * LANES)
            qa_ref[:, out_cols] = jnp.where(lane < dh, qh, q_tail).astype(qa_ref.dtype)
            ka_ref[:, out_cols] = jnp.where(lane < dh, kh, k_tail).astype(ka_ref.dtype)
            pick = ((vlane == a * dh + vrow) & (vrow < dh)).astype(BF16)
            vt = lax.dot_general(pick, vb, _NT, preferred_element_type=F32)
            vt_ref[0, h] = jnp.where(out_row == dh, 1.0, vt).astype(vt_ref.dtype)


def _fox_prep(h_main, hf, bf_row, qg_row, kg_row, *, batch, seq, heads, dh, rows):
    m = h_main.shape[0]
    w = heads * dh
    nt = seq // rows

    def col(j):
        return pl.BlockSpec((rows, w), lambda b, t: (b * nt + t, j))

    def row(n):
        return pl.BlockSpec((1, n), lambda b, t: (0, 0))

    aug = pl.BlockSpec((rows, heads * LANES), lambda b, t: (b * nt + t, 0))
    return pl.pallas_call(
        functools.partial(_fox_prep_kernel, rows=rows, heads=heads, dh=dh),
        grid=(batch, nt),
        in_specs=[col(0), col(1), col(2),
                  pl.BlockSpec((rows, LANES), lambda b, t: (b * nt + t, 0)),
                  row(LANES), row(w), row(w)],
        out_specs=[aug, aug,
                   pl.BlockSpec((1, heads, FOX_VROWS, rows), lambda b, t: (b, 0, 0, t))],
        out_shape=[jax.ShapeDtypeStruct((m, heads * LANES), BF16),
                   jax.ShapeDtypeStruct((m, heads * LANES), BF16),
                   jax.ShapeDtypeStruct((batch, heads, FOX_VROWS, seq), BF16)],
        scratch_shapes=[pltpu.VMEM((1, LANES), F32)],
        compiler_params=_params("parallel", "arbitrary"),
        name="fox_prep",
    )(h_main, h_main, h_main, hf, bf_row, qg_row, kg_row)


def _fox_attn_kernel(q_ref, k_ref, v_ref, z_ref, o_ref, *, tile, dh, per_block):
    qi = pl.program_id(2)
    below_diag = _iota2((tile, tile), 0) <= _iota2((tile, tile), 1)
    q_heads = [q_ref[:, a * LANES:(a + 1) * LANES] for a in range(per_block)]

    def step(kj, carry, masked):
        start = pl.multiple_of(kj * tile, tile)
        new = []
        logits = [lax.dot_general(k_ref[pl.ds(start, tile), a * LANES:(a + 1) * LANES], q_heads[a], _NT,
                                  preferred_element_type=F32) for a in range(per_block)]
        for a in range(per_block):
            m_prev, acc = carry[a]
            s = logits[a]
            if masked:
                s = jnp.where(below_diag, s, -jnp.inf)
            m_new = jnp.maximum(m_prev, jnp.max(s, axis=0, keepdims=True))
            p = jnp.exp2(s - m_new).astype(BF16)
            vt = v_ref[0, a, :, pl.ds(start, tile)]
            acc = jnp.exp2(m_prev - m_new) * acc + jnp.dot(vt, p, preferred_element_type=F32)
            new.append((m_new, acc))
        return tuple(new)

    init = tuple((jnp.full((1, tile), -1e30, F32), jnp.zeros((FOX_VROWS, tile), F32))
                 for _ in range(per_block))
    carry = lax.fori_loop(0, qi, functools.partial(step, masked=False), init)
    carry = step(qi, carry, True)
    o_t = jnp.concatenate([acc[:dh] / acc[dh:dh + 1] for _, acc in carry], axis=0)
    o_ref[...] = (o_t.T * _silu(z_ref[...])).astype(o_ref.dtype)


def _fox_attn(qa, ka, vt, h_main, *, batch, seq, heads, dh, tile):
    m = qa.shape[0]
    nq = seq // tile
    per_block = LANES // dh
    blocks = heads // per_block
    z_off = 3 * blocks
    wide = per_block * LANES
    return pl.pallas_call(
        functools.partial(_fox_attn_kernel, tile=tile, dh=dh, per_block=per_block),
        grid=(batch, blocks, nq),
        in_specs=[pl.BlockSpec((tile, wide), lambda b, h, i: (b * nq + i, h)),
                  pl.BlockSpec((seq, wide), lambda b, h, i: (b, h)),
                  pl.BlockSpec((1, per_block, FOX_VROWS, seq), lambda b, h, i: (b, h, 0, 0)),
                  pl.BlockSpec((tile, LANES), lambda b, h, i: (b * nq + i, z_off + h))],
        out_specs=pl.BlockSpec((tile, LANES), lambda b, h, i: (b * nq + i, h)),
        out_shape=jax.ShapeDtypeStruct((m, heads * dh), BF16),
        compiler_params=_params("parallel", "parallel", "arbitrary"),
        name="fox_attn",
    )(qa, ka, vt, h_main)


def _post_kernel(a_ref, x_ref, p_ref, wo_ref, g_ref, b_ref, wg_ref, wp_ref, o_ref, *, alpha):
    t = alpha * x_ref[...] + jnp.dot(a_ref[...], wo_ref[...], preferred_element_type=F32)
    mu = jnp.mean(t, axis=-1, keepdims=True)
    d = t - mu
    var = jnp.mean(d * d, axis=-1, keepdims=True)
    xn = d * lax.rsqrt(var + LN_EPS) * g_ref[...] + b_ref[...]
    gate = _sigmoid(_mm(xn, wg_ref[...]))
    o_ref[...] = xn + gate * _mm(p_ref[...], wp_ref[...])


def _post(a, x, p, w_out, ln_g, ln_b, w_gate, w_proj, *, alpha, tm):
    m, d = x.shape
    dp = p.shape[1]

    def rows(n):
        return pl.BlockSpec((tm, n), lambda i: (i, 0))

    def full(r, n):
        return pl.BlockSpec((r, n), lambda i: (0, 0))

    return pl.pallas_call(
        functools.partial(_post_kernel, alpha=alpha),
        grid=(m // tm,),
        in_specs=[rows(a.shape[1]), rows(d), rows(dp), full(a.shape[1], d), full(1, d), full(1, d),
                  full(d, d), full(dp, d)],
        out_specs=rows(d),
        out_shape=jax.ShapeDtypeStruct((m, d), F32),
        compiler_params=_params("parallel"),
        name="post_block",
    )(a, x, p, w_out, ln_g, ln_b, w_gate, w_proj)


def _pad_cols(w, n):
    return jnp.pad(w, ((0, 0), (0, n - w.shape[1])))


def _pad_row(v, n):
    return jnp.pad(v, (0, n - v.shape[0])).reshape(1, n)


def _gdn_layer(x, w_in, conv_w, a_log, dt_bias, norm_g, *, batch, seq):
    heads = a_log.shape[0]
    dk = norm_g.shape[0]
    wide = 4 * heads * dk
    h_main = _matmul(x, w_in[:, :wide].astype(BF16), tm=512, tn=1024, out_dtype=F32, name="gdn_in_proj")
    w_gates = jnp.concatenate([_pad_cols(w_in[:, wide:wide + heads], LANES),
                               _pad_cols(w_in[:, wide + heads:], LANES)], axis=1).astype(BF16)
    hs = _matmul(x, w_gates, tm=512, tn=2 * LANES, out_dtype=F32, name="gdn_gate_proj")
    beta, gcum, gcum_t = _gdn_gates(hs, _pad_row(a_log, LANES), _pad_row(dt_bias, LANES), rows=512)
    return _gdn_core(h_main, conv_w, beta, gcum, gcum_t, norm_g.reshape(1, dk),
                     batch=batch, seq=seq, heads=heads, dk=dk, rows=512)


def _fox_layer(x, w_in, b_f, q_norm_g, k_norm_g, *, batch, seq):
    heads = b_f.shape[0]
    dh = q_norm_g.shape[0]
    wide = 4 * heads * dh
    h_main = _matmul(x, w_in[:, :wide].astype(BF16), tm=512, tn=1024, out_dtype=F32, name="fox_in_proj")
    hf = _matmul(x, _pad_cols(w_in[:, wide:], LANES).astype(BF16), tm=512, tn=LANES, out_dtype=F32,
                 name="fox_forget_proj")
    qa, ka, vt = _fox_prep(h_main, hf, _pad_row(b_f, LANES),
                           jnp.tile(q_norm_g, heads).reshape(1, heads * dh),
                           jnp.tile(k_norm_g, heads).reshape(1, heads * dh),
                           batch=batch, seq=seq, heads=heads, dh=dh, rows=512)
    return _fox_attn(qa, ka, vt, h_main, batch=batch, seq=seq, heads=heads, dh=dh, tile=512)


def kernel(x, p, ln_g, ln_b, ple_w_gate, ple_w_proj, gdn_w_in, gdn_conv_w, gdn_a_log, gdn_dt_bias,
           gdn_norm_g, gdn_w_out, fox_w_in, fox_b_f, fox_q_norm_g, fox_k_norm_g, fox_w_out):
    batch, seq, d = x.shape
    depth = ln_g.shape[0]
    alpha = (2 * depth) ** 0.25
    xf = x.reshape(batch * seq, d)
    for i in range(depth):
        j = i // 2
        if i % 2 == 0:
            a = _gdn_layer(xf, gdn_w_in[j], gdn_conv_w[j], gdn_a_log[j], gdn_dt_bias[j], gdn_norm_g[j],
                           batch=batch, seq=seq)
            w_out = gdn_w_out[j]
        else:
            a = _fox_layer(xf, fox_w_in[j], fox_b_f[j], fox_q_norm_g[j], fox_k_norm_g[j],
                           batch=batch, seq=seq)
            w_out = fox_w_out[j]
        xf = _post(a, xf, p[i].reshape(batch * seq, -1), w_out.astype(BF16), ln_g[i].reshape(1, d),
                   ln_b[i].reshape(1, d), ple_w_gate[i].astype(BF16), ple_w_proj[i].astype(BF16),
                   alpha=alpha, tm=512)
    return xf.reshape(batch, seq, d)
```

```python
import functools

import jax
import jax.numpy as jnp
from jax import lax
from jax.experimental import pallas as pl
from jax.experimental.pallas import tpu as pltpu

F32 = jnp.float32
BF16 = jnp.bfloat16
HIGHEST = lax.Precision.HIGHEST

LANES = 128
V7X_VMEM_BYTES = 64 * 1024 * 1024
VMEM_LIMIT = V7X_VMEM_BYTES * 3 // 4

GDN_CHUNK = 64
LN_EPS = 1e-5
RMS_EPS = 1e-6
LOG2E = 1.4426950408889634

_NT = (((1,), (1,)), ((), ()))
_TN = (((0,), (0,)), ((), ()))


def _mm(a, b):
    return jnp.dot(a.astype(BF16), b.astype(BF16), preferred_element_type=F32)


def _mm_f32(a, b):
    return jnp.dot(a, b, precision=HIGHEST, preferred_element_type=F32)


def _sigmoid(x):
    return 1.0 / (1.0 + jnp.exp(-x))


def _silu(x):
    return x * _sigmoid(x)


def _softplus(x):
    return jnp.maximum(x, 0.0) + jnp.log1p(jnp.exp(-jnp.abs(x)))


def _iota2(shape, dim):
    return lax.broadcasted_iota(jnp.int32, shape, dim)


def _params(*semantics):
    return pltpu.CompilerParams(dimension_semantics=semantics, vmem_limit_bytes=VMEM_LIMIT)


def _in_proj_kernel(x_ref, wm_ref, ws_ref, hm_ref, hs_ref, *, tn):
    xb = x_ref[...].astype(BF16)
    for j in range(wm_ref.shape[1] // tn):
        cols = slice(j * tn, (j + 1) * tn)
        hm_ref[:, cols] = jnp.dot(xb, wm_ref[:, cols], preferred_element_type=F32).astype(hm_ref.dtype)
    hs_ref[...] = jnp.dot(xb, ws_ref[...], preferred_element_type=F32)


def _in_proj(x, w_main, w_small, *, tm, tn, name):
    m, k = x.shape
    n = w_main.shape[1]
    ns = w_small.shape[1]
    return pl.pallas_call(
        functools.partial(_in_proj_kernel, tn=tn),
        grid=(m // tm,),
        in_specs=[pl.BlockSpec((tm, k), lambda i: (i, 0)),
                  pl.BlockSpec((k, n), lambda i: (0, 0)),
                  pl.BlockSpec((k, ns), lambda i: (0, 0))],
        out_specs=[pl.BlockSpec((tm, n), lambda i: (i, 0)),
                   pl.BlockSpec((tm, ns), lambda i: (i, 0))],
        out_shape=[jax.ShapeDtypeStruct((m, n), BF16), jax.ShapeDtypeStruct((m, ns), F32)],
        compiler_params=_params("parallel"),
        name=name,
    )(x, w_main, w_small)


def _gdn_gate_kernel(hs_ref, alog_ref, dtb_ref, beta_ref, gcum_ref, gcum_t_ref, *, rows):
    c = GDN_CHUNK
    beta_ref[...] = _sigmoid(hs_ref[:, :LANES])
    g = -jnp.exp(alog_ref[...]) * _softplus(hs_ref[:, LANES:] + dtb_ref[...])
    tril = (_iota2((c, c), 0) >= _iota2((c, c), 1)).astype(F32)
    eye = (_iota2((8, LANES), 0) == _iota2((8, LANES), 1)).astype(F32)
    for i in range(rows // c):
        gc = _mm_f32(tril, g[i * c:(i + 1) * c])
        gcum_ref[i * c:(i + 1) * c, :] = gc
        gcum_t_ref[i] = lax.dot_general(eye, gc, _NT, precision=HIGHEST, preferred_element_type=F32)


def _gdn_gates(hs, alog_row, dtb_row, *, rows):
    m = hs.shape[0]
    c = GDN_CHUNK
    return pl.pallas_call(
        functools.partial(_gdn_gate_kernel, rows=rows),
        grid=(m // rows,),
        in_specs=[pl.BlockSpec((rows, 2 * LANES), lambda i: (i, 0)),
                  pl.BlockSpec((1, LANES), lambda i: (0, 0)),
                  pl.BlockSpec((1, LANES), lambda i: (0, 0))],
        out_specs=[pl.BlockSpec((rows, LANES), lambda i: (i, 0)),
                   pl.BlockSpec((rows, LANES), lambda i: (i, 0)),
                   pl.BlockSpec((rows // c, 8, c), lambda i: (i, 0, 0))],
        out_shape=[jax.ShapeDtypeStruct((m, LANES), F32),
                   jax.ShapeDtypeStruct((m, LANES), F32),
                   jax.ShapeDtypeStruct((m // c, 8, c), F32)],
        compiler_params=_params("parallel"),
        name="gdn_gates",
    )(hs, alog_row, dtb_row)


def _gdn_core_kernel(hq_ref, hk_ref, hv_ref, hz_ref, wq_ref, wk_ref, wv_ref, beta_ref, gcum_ref,
                     gcum_t_ref, ng_ref, o_ref, state_ref, xe_ref, *, rows, dk, hb):
    c = GDN_CHUNK
    n = rows // c
    group = pl.program_id(1)

    @pl.when(pl.program_id(2) == 0)
    def _():
        state_ref[...] = jnp.zeros_like(state_ref)
        xe_ref[:, :8, :] = jnp.zeros((3, 8, xe_ref.shape[2]), F32)

    def conv_silu(x_ref, w_ref, slot):
        w = w_ref[...]
        taps = w.shape[0]
        xe_ref[slot, 8:, :] = x_ref[...].astype(F32)
        y = w[taps - 1:taps, :] * xe_ref[slot, 8:, :]
        for t in range(taps - 1):
            y = y + w[t:t + 1, :] * xe_ref[slot, pl.ds(8 - taps + 1 + t, rows), :]
        xe_ref[slot, :8, :] = xe_ref[slot, rows:, :]
        return _silu(y)

    def l2norm(x):
        return x * lax.rsqrt(jnp.sum(x * x, axis=-1, keepdims=True) + RMS_EPS)

    k_all = conv_silu(hk_ref, wk_ref, 1)
    q_all = conv_silu(hq_ref, wq_ref, 0)
    v_all = conv_silu(hv_ref, wv_ref, 2)
    lane = _iota2((rows, LANES), 1)
    ri = _iota2((c, c), 0)
    ci = _iota2((c, c), 1)
    causal = ri >= ci
    strict = ri > ci
    eye = (ri == ci).astype(F32)
    dv = dk

    units = []
    for a in range(hb):
        head = group * hb + a
        cols = slice(a * dk, (a + 1) * dk)
        k = l2norm(k_all[:, cols])
        q = l2norm(q_all[:, cols]) * (dk ** -0.5)
        v = v_all[:, cols]
        beta = jnp.sum(jnp.where(lane == head, beta_ref[...], 0.0), axis=1, keepdims=True)
        gcol = jnp.sum(jnp.where(lane == head, gcum_ref[...], 0.0), axis=1, keepdims=True)
        for i in range(n):
            sl = slice(i * c, (i + 1) * c)
            units.append(dict(a=a, q=q[sl], k=k[sl], v=v[sl], beta=beta[sl], g=gcol[sl],
                              grow=gcum_t_ref[i, pl.ds(head, 1), :]))
    for u in units:
        u["glast"] = u["g"][c - 1:c, :]
        u["eg"] = jnp.exp(u["g"])
        u["decay"] = jnp.where(causal, jnp.exp(jnp.where(causal, u["g"] - u["grow"], 0.0)), 0.0)
        u["kb"] = u["k"] * u["beta"]
    for u in units:
        kbf = u["k"].astype(BF16)
        u["kk"] = lax.dot_general(u["kb"].astype(BF16), kbf, _NT, preferred_element_type=F32)
        u["qk"] = lax.dot_general(u["q"].astype(BF16), kbf, _NT, preferred_element_type=F32)
    for u in units:
        u["lmat"] = jnp.where(strict, u["kk"] * u["decay"], 0.0)
        u["amat"] = jnp.where(causal, u["qk"] * u["decay"], 0.0)
        u["tinv"] = eye - u["lmat"]
    for u in units:
        u["pk"] = _mm(u["lmat"], u["lmat"])
    span = 4
    while span < c:
        for u in units:
            u["both"] = _mm(u["pk"], jnp.concatenate([u["tinv"], u["pk"]], axis=1))
        for u in units:
            u["tinv"] = u["tinv"] + u["both"][:, :c]
            u["pk"] = u["both"][:, c:]
        span *= 2
    for u in units:
        u["tinv"] = u["tinv"] + _mm(u["pk"], u["tinv"])
    for u in units:
        u["x"] = _mm(u["tinv"], jnp.concatenate([u["v"] * u["beta"], u["kb"] * u["eg"]], axis=1))
    for u in units:
        u["ax"] = _mm(u["amat"], u["x"])
        kd = u["k"] * jnp.exp(u["glast"] - u["g"])
        u["kx"] = lax.dot_general(kd.astype(BF16), u["x"].astype(BF16), _TN, preferred_element_type=F32)
    for u in units:
        u["qeff"] = u["q"] * u["eg"] - u["ax"][:, dv:]
    states = [state_ref[a] for a in range(hb)]
    outs = [[] for _ in range(hb)]
    for i in range(n):
        for a in range(hb):
            u = units[a * n + i]
            sb = states[a].astype(BF16)
            outs[a].append(_mm(u["qeff"], sb) + u["ax"][:, :dv])
            states[a] = jnp.exp(u["glast"]) * states[a] + u["kx"][:, :dv] - _mm(u["kx"][:, dv:], sb)
    for a in range(hb):
        state_ref[a] = states[a]
        o = jnp.concatenate(outs[a], axis=0)
        o = o * lax.rsqrt(jnp.mean(o * o, axis=-1, keepdims=True) + RMS_EPS) * ng_ref[...]
        cols = slice(a * dk, (a + 1) * dk)
        o_ref[:, cols] = (o * _silu(hz_ref[:, cols].astype(F32))).astype(o_ref.dtype)


def _gdn_core(h_main, conv_w, beta, gcum, gcum_t, ng_row, *, batch, seq, heads, dk, rows, hb):
    m = h_main.shape[0]
    nb = seq // rows
    c = GDN_CHUNK
    taps = conv_w.shape[0]
    groups = heads // hb
    wide = hb * dk

    def col(off):
        return pl.BlockSpec((rows, wide), lambda b, h, j: (b * nb + j, off + h))

    def wcol(off):
        return pl.BlockSpec((taps, wide), lambda b, h, j: (0, off + h))

    def gate():
        return pl.BlockSpec((rows, LANES), lambda b, h, j: (b * nb + j, 0))

    return pl.pallas_call(
        functools.partial(_gdn_core_kernel, rows=rows, dk=dk, hb=hb),
        grid=(batch, groups, nb),
        in_specs=[col(0), col(groups), col(2 * groups), col(3 * groups),
                  wcol(0), wcol(groups), wcol(2 * groups),
                  gate(), gate(),
                  pl.BlockSpec((rows // c, 8, c), lambda b, h, j: (b * nb + j, 0, 0)),
                  pl.BlockSpec((1, dk), lambda b, h, j: (0, 0))],
        out_specs=pl.BlockSpec((rows, wide), lambda b, h, j: (b * nb + j, h)),
        out_shape=jax.ShapeDtypeStruct((m, heads * dk), BF16),
        scratch_shapes=[pltpu.VMEM((hb, dk, dk), F32), pltpu.VMEM((3, rows + 8, wide), F32)],
        compiler_params=_params("parallel", "parallel", "arbitrary"),
        name="gdn_core",
    )(h_main, h_main, h_main, h_main, conv_w, conv_w, conv_w, beta, gcum, gcum_t, ng_row)


FOX_VROWS = 80


def _split3(x):
    hi = x.astype(BF16).astype(F32)
    r = x - hi
    mid = r.astype(BF16).astype(F32)
    lo = (r - mid).astype(BF16).astype(F32)
    return hi, mid, lo


def _fox_prep_kernel(hq_ref, hk_ref, hv_ref, hf_ref, bf_ref, qg_ref, kg_ref,
                     qa_ref, ka_ref, vt_ref, carry_ref, *, rows, heads, dh):
    @pl.when(pl.program_id(1) == 0)
    def _():
        carry_ref[...] = jnp.zeros_like(carry_ref)

    logf = -_softplus(-(hf_ref[...] + bf_ref[...]))
    lane = _iota2((rows, LANES), 1)
    logf = jnp.where(lane < heads, logf, 0.0)
    tril = (_iota2((rows, rows), 0) >= _iota2((rows, rows), 1)).astype(F32)
    c = _mm_f32(tril, logf) + carry_ref[...]
    carry_ref[...] = c[rows - 1:rows, :]
    c2 = c * LOG2E

    same_head = (_iota2((LANES, LANES), 0) // dh == _iota2((LANES, LANES), 1) // dh).astype(BF16)

    def head_rms(x, g, scale):
        sq = x * x
        hi = sq.astype(BF16)
        lo = (sq - hi.astype(F32)).astype(BF16)
        ssum = (jnp.dot(hi, same_head, preferred_element_type=F32)
                + jnp.dot(lo, same_head, preferred_element_type=F32))
        return x * lax.rsqrt(ssum * (1.0 / dh) + RMS_EPS) * (g * scale)

    per_block = LANES // dh
    bias_lane = dh
    vrow = _iota2((FOX_VROWS, LANES), 0)
    vlane = _iota2((FOX_VROWS, LANES), 1)
    out_row = _iota2((FOX_VROWS, rows), 0)
    for j in range(heads // per_block):
        cols = slice(j * LANES, (j + 1) * LANES)
        qn = head_rms(hq_ref[:, cols].astype(F32), qg_ref[:, cols], dh ** -0.5 * LOG2E)
        kn = head_rms(hk_ref[:, cols].astype(F32), kg_ref[:, cols], 1.0)
        vb = hv_ref[:, cols].astype(BF16)
        for a in range(per_block):
            h = j * per_block + a
            chi, cmid, clo = _split3(jnp.sum(jnp.where(lane == h, c2, 0.0), axis=1, keepdims=True))
            qh = qn if a == 0 else pltpu.roll(qn, LANES - a * dh, axis=1)
            kh = kn if a == 0 else pltpu.roll(kn, LANES - a * dh, axis=1)
            q_tail = jnp.where(lane == bias_lane, chi, jnp.where(lane == bias_lane + 1, cmid, jnp.where(
                lane == bias_lane + 2, clo, jnp.where(lane < bias_lane + 6, 1.0, 0.0))))
            k_tail = jnp.where(lane < bias_lane + 3, 1.0, jnp.where(lane == bias_lane + 3, -chi, jnp.where(
                lane == bias_lane + 4, -cmid, jnp.where(lane == bias_lane + 5, -clo, 0.0))))
            out_cols = slice(h * LANES, (h + 1) * LANES)
            qa_ref[:, out_cols] = jnp.where(lane < dh, qh, q_tail).astype(qa_ref.dtype)
            ka_ref[:, out_cols] = jnp.where(lane < dh, kh, k_tail).astype(ka_ref.dtype)
            pick = ((vlane == a * dh + vrow) & (vrow < dh)).astype(BF16)
            vt = lax.dot_general(pick, vb, _NT, preferred_element_type=F32)
            vt_ref[0, h] = jnp.where(out_row == dh, 1.0, vt).astype(vt_ref.dtype)


def _fox_prep(h_main, hf, bf_row, qg_row, kg_row, *, batch, seq, heads, dh, rows):
    m = h_main.shape[0]
    w = heads * dh
    nt = seq // rows

    def col(j):
        return pl.BlockSpec((rows, w), lambda b, t: (b * nt + t, j))

    def row(n):
        return pl.BlockSpec((1, n), lambda b, t: (0, 0))

    aug = pl.BlockSpec((rows, heads * LANES), lambda b, t: (b * nt + t, 0))
    return pl.pallas_call(
        functools.partial(_fox_prep_kernel, rows=rows, heads=heads, dh=dh),
        grid=(batch, nt),
        in_specs=[col(0), col(1), col(2),
                  pl.BlockSpec((rows, LANES), lambda b, t: (b * nt + t, 0)),
                  row(LANES), row(w), row(w)],
        out_specs=[aug, aug,
                   pl.BlockSpec((1, heads, FOX_VROWS, rows), lambda b, t: (b, 0, 0, t))],
        out_shape=[jax.ShapeDtypeStruct((m, heads * LANES), BF16),
                   jax.ShapeDtypeStruct((m, heads * LANES), BF16),
                   jax.ShapeDtypeStruct((batch, heads, FOX_VROWS, seq), BF16)],
        scratch_shapes=[pltpu.VMEM((1, LANES), F32)],
        compiler_params=_params("parallel", "arbitrary"),
        name="fox_prep",
    )(h_main, h_main, h_main, hf, bf_row, qg_row, kg_row)


def _fox_attn_kernel(q_ref, k_ref, v_ref, z_ref, o_ref, s0_ref, s1_ref, m_ref, acc_ref, *, tile, dh, per_block):
    qi = pl.program_id(2)
    below_diag = _iota2((tile, tile), 0) <= _iota2((tile, tile), 1)

    def logits(kj, s_ref):
        start = pl.multiple_of(kj * tile, tile)
        for a in range(per_block):
            s_ref[a] = lax.dot_general(k_ref[pl.ds(start, tile), a * LANES:(a + 1) * LANES],
                                       q_ref[:, a * LANES:(a + 1) * LANES], _NT,
                                       preferred_element_type=F32)

    def softmax_pv(kj, s_ref, masked):
        start = pl.multiple_of(kj * tile, tile)
        for a in range(per_block):
            s = s_ref[a]
            if masked:
                s = jnp.where(below_diag, s, -jnp.inf)
            m_prev = m_ref[a]
            m_new = jnp.maximum(m_prev, jnp.max(s, axis=0, keepdims=True))
            p = jnp.exp2(s - m_new).astype(BF16)
            vt = v_ref[0, a, :, pl.ds(start, tile)]
            acc_ref[a] = jnp.exp2(m_prev - m_new) * acc_ref[a] + jnp.dot(vt, p, preferred_element_type=F32)
            m_ref[a] = m_new

    m_ref[...] = jnp.full(m_ref.shape, -1e30, F32)
    acc_ref[...] = jnp.zeros(acc_ref.shape, F32)
    logits(0, s0_ref)

    def two_tiles(u, _):
        kj = 2 * u
        logits(kj + 1, s1_ref)
        softmax_pv(kj, s0_ref, False)
        logits(kj + 2, s0_ref)
        softmax_pv(kj + 1, s1_ref, False)
        return 0

    lax.fori_loop(0, qi // 2, two_tiles, 0)
    odd = qi % 2 == 1

    @pl.when(odd)
    def _():
        logits(qi, s1_ref)
        softmax_pv(qi - 1, s0_ref, False)
        softmax_pv(qi, s1_ref, True)

    @pl.when(jnp.logical_not(odd))
    def _():
        softmax_pv(qi, s0_ref, True)

    o_t = jnp.concatenate([acc_ref[a, :dh, :] / acc_ref[a, dh:dh + 1, :] for a in range(per_block)], axis=0)
    o_ref[...] = (o_t.T * _silu(z_ref[...].astype(F32))).astype(o_ref.dtype)


def _fox_attn(qa, ka, vt, h_main, *, batch, seq, heads, dh, tile):
    m = qa.shape[0]
    nq = seq // tile
    per_block = LANES // dh
    blocks = heads // per_block
    z_off = 3 * blocks
    wide = per_block * LANES
    return pl.pallas_call(
        functools.partial(_fox_attn_kernel, tile=tile, dh=dh, per_block=per_block),
        grid=(batch, blocks, nq),
        in_specs=[pl.BlockSpec((tile, wide), lambda b, h, i: (b * nq + i, h)),
                  pl.BlockSpec((seq, wide), lambda b, h, i: (b, h)),
                  pl.BlockSpec((1, per_block, FOX_VROWS, seq), lambda b, h, i: (b, h, 0, 0)),
                  pl.BlockSpec((tile, LANES), lambda b, h, i: (b * nq + i, z_off + h))],
        out_specs=pl.BlockSpec((tile, LANES), lambda b, h, i: (b * nq + i, h)),
        out_shape=jax.ShapeDtypeStruct((m, heads * dh), BF16),
        scratch_shapes=[pltpu.VMEM((per_block, tile, tile), F32), pltpu.VMEM((per_block, tile, tile), F32),
                        pltpu.VMEM((per_block, 1, tile), F32), pltpu.VMEM((per_block, FOX_VROWS, tile), F32)],
        compiler_params=_params("parallel", "parallel", "arbitrary"),
        name="fox_attn",
    )(qa, ka, vt, h_main)


def _post_kernel(a_ref, x_ref, p_ref, wo_ref, g_ref, b_ref, wg_ref, wp_ref, o_ref, *, alpha):
    t = alpha * x_ref[...] + jnp.dot(a_ref[...], wo_ref[...], preferred_element_type=F32)
    mu = jnp.mean(t, axis=-1, keepdims=True)
    d = t - mu
    var = jnp.mean(d * d, axis=-1, keepdims=True)
    xn = d * lax.rsqrt(var + LN_EPS) * g_ref[...] + b_ref[...]
    gate = _sigmoid(_mm(xn, wg_ref[...]))
    o_ref[...] = xn + gate * _mm(p_ref[...], wp_ref[...])


def _post(a, x, p, w_out, ln_g, ln_b, w_gate, w_proj, *, alpha, tm):
    m, d = x.shape
    dp = p.shape[1]

    def rows(n):
        return pl.BlockSpec((tm, n), lambda i: (i, 0))

    def full(r, n):
        return pl.BlockSpec((r, n), lambda i: (0, 0))

    return pl.pallas_call(
        functools.partial(_post_kernel, alpha=alpha),
        grid=(m // tm,),
        in_specs=[rows(a.shape[1]), rows(d), rows(dp), full(a.shape[1], d), full(1, d), full(1, d),
                  full(d, d), full(dp, d)],
        out_specs=rows(d),
        out_shape=jax.ShapeDtypeStruct((m, d), F32),
        compiler_params=_params("parallel"),
        name="post_block",
    )(a, x, p, w_out, ln_g, ln_b, w_gate, w_proj)


def _pad_cols(w, n):
    return jnp.pad(w, ((0, 0), (0, n - w.shape[1])))


def _pad_row(v, n):
    return jnp.pad(v, (0, n - v.shape[0])).reshape(1, n)


def _gdn_layer(x, w_in, conv_w, a_log, dt_bias, norm_g, *, batch, seq):
    heads = a_log.shape[0]
    dk = norm_g.shape[0]
    wide = 4 * heads * dk
    w_gates = jnp.concatenate([_pad_cols(w_in[:, wide:wide + heads], LANES),
                               _pad_cols(w_in[:, wide + heads:], LANES)], axis=1).astype(BF16)
    h_main, hs = _in_proj(x, w_in[:, :wide].astype(BF16), w_gates, tm=512, tn=1024, name="gdn_in_proj")
    beta, gcum, gcum_t = _gdn_gates(hs, _pad_row(a_log, LANES), _pad_row(dt_bias, LANES), rows=512)
    return _gdn_core(h_main, conv_w, beta, gcum, gcum_t, norm_g.reshape(1, dk),
                     batch=batch, seq=seq, heads=heads, dk=dk, rows=512, hb=4)


def _fox_layer(x, w_in, b_f, q_norm_g, k_norm_g, *, batch, seq):
    heads = b_f.shape[0]
    dh = q_norm_g.shape[0]
    wide = 4 * heads * dh
    h_main, hf = _in_proj(x, w_in[:, :wide].astype(BF16), _pad_cols(w_in[:, wide:], LANES).astype(BF16),
                          tm=512, tn=1024, name="fox_in_proj")
    qa, ka, vt = _fox_prep(h_main, hf, _pad_row(b_f, LANES),
                           jnp.tile(q_norm_g, heads).reshape(1, heads * dh),
                           jnp.tile(k_norm_g, heads).reshape(1, heads * dh),
                           batch=batch, seq=seq, heads=heads, dh=dh, rows=512)
    return _fox_attn(qa, ka, vt, h_main, batch=batch, seq=seq, heads=heads, dh=dh, tile=512)


def kernel(x, p, ln_g, ln_b, ple_w_gate, ple_w_proj, gdn_w_in, gdn_conv_w, gdn_a_log, gdn_dt_bias,
           gdn_norm_g, gdn_w_out, fox_w_in, fox_b_f, fox_q_norm_g, fox_k_norm_g, fox_w_out):
    batch, seq, d = x.shape
    depth = ln_g.shape[0]
    alpha = (2 * depth) ** 0.25
    xf = x.reshape(batch * seq, d)
    for i in range(depth):
        j = i // 2
        if i % 2 == 0:
            a = _gdn_layer(xf, gdn_w_in[j], gdn_conv_w[j], gdn_a_log[j], gdn_dt_bias[j], gdn_norm_g[j],
                           batch=batch, seq=seq)
            w_out = gdn_w_out[j]
        else:
            a = _fox_layer(xf, fox_w_in[j], fox_b_f[j], fox_q_norm_g[j], fox_k_norm_g[j],
                           batch=batch, seq=seq)
            w_out = fox_w_out[j]
        xf = _post(a, xf, p[i].reshape(batch * seq, -1), w_out.astype(BF16), ln_g[i].reshape(1, d),
                   ln_b[i].reshape(1, d), ple_w_gate[i].astype(BF16), ple_w_proj[i].astype(BF16),
                   alpha=alpha, tm=512)
    return xf.reshape(batch, seq, d)
```

```python
import functools

import jax
import jax.numpy as jnp
import numpy as np
from jax import lax
from jax.experimental import pallas as pl
from jax.experimental.pallas import tpu as pltpu

F32 = jnp.float32
BF16 = jnp.bfloat16
HIGHEST = lax.Precision.HIGHEST

LANES = 128
V7X_VMEM_BYTES = 64 * 1024 * 1024
VMEM_LIMIT = V7X_VMEM_BYTES * 3 // 4

GDN_CHUNK = 64
LN_EPS = 1e-5
RMS_EPS = 1e-6
LOG2E = 1.4426950408889634

_NT = (((1,), (1,)), ((), ()))
_TN = (((0,), (0,)), ((), ()))


def _mm(a, b):
    return jnp.dot(a.astype(BF16), b.astype(BF16), preferred_element_type=F32)


def _mm_f32(a, b):
    return jnp.dot(a, b, precision=HIGHEST, preferred_element_type=F32)


def _sigmoid(x):
    return 1.0 / (1.0 + jnp.exp(-x))


def _silu(x):
    return x * _sigmoid(x)


def _softplus(x):
    return jnp.maximum(x, 0.0) + jnp.log1p(jnp.exp(-jnp.abs(x)))


def _iota2(shape, dim):
    return lax.broadcasted_iota(jnp.int32, shape, dim)


def _params(*semantics):
    return pltpu.CompilerParams(dimension_semantics=semantics, vmem_limit_bytes=VMEM_LIMIT)


def _in_proj_kernel(x_ref, wm_ref, ws_ref, hm_ref, hs_ref, *, tn):
    xb = x_ref[...].astype(BF16)
    for j in range(wm_ref.shape[1] // tn):
        cols = slice(j * tn, (j + 1) * tn)
        hm_ref[:, cols] = jnp.dot(xb, wm_ref[:, cols], preferred_element_type=F32).astype(hm_ref.dtype)
    hs_ref[...] = jnp.dot(xb, ws_ref[...], preferred_element_type=F32)


def _in_proj(x, w_main, w_small, *, tm, tn, name):
    m, k = x.shape
    n = w_main.shape[1]
    ns = w_small.shape[1]
    return pl.pallas_call(
        functools.partial(_in_proj_kernel, tn=tn),
        grid=(m // tm,),
        in_specs=[pl.BlockSpec((tm, k), lambda i: (i, 0)),
                  pl.BlockSpec((k, n), lambda i: (0, 0)),
                  pl.BlockSpec((k, ns), lambda i: (0, 0))],
        out_specs=[pl.BlockSpec((tm, n), lambda i: (i, 0)),
                   pl.BlockSpec((tm, ns), lambda i: (i, 0))],
        out_shape=[jax.ShapeDtypeStruct((m, n), BF16), jax.ShapeDtypeStruct((m, ns), F32)],
        compiler_params=_params("parallel"),
        name=name,
    )(x, w_main, w_small)


def _gdn_gate_kernel(hs_ref, alog_ref, dtb_ref, beta_ref, gcum_ref, gcum_t_ref, *, rows):
    c = GDN_CHUNK
    beta_ref[...] = _sigmoid(hs_ref[:, :LANES])
    g = -jnp.exp(alog_ref[...]) * _softplus(hs_ref[:, LANES:] + dtb_ref[...])
    tril = (_iota2((c, c), 0) >= _iota2((c, c), 1)).astype(F32)
    eye = (_iota2((8, LANES), 0) == _iota2((8, LANES), 1)).astype(F32)
    for i in range(rows // c):
        gc = _mm_f32(tril, g[i * c:(i + 1) * c])
        gcum_ref[i * c:(i + 1) * c, :] = gc
        gcum_t_ref[i] = lax.dot_general(eye, gc, _NT, precision=HIGHEST, preferred_element_type=F32)


def _gdn_gates(hs, alog_row, dtb_row, *, rows):
    m = hs.shape[0]
    c = GDN_CHUNK
    return pl.pallas_call(
        functools.partial(_gdn_gate_kernel, rows=rows),
        grid=(m // rows,),
        in_specs=[pl.BlockSpec((rows, 2 * LANES), lambda i: (i, 0)),
                  pl.BlockSpec((1, LANES), lambda i: (0, 0)),
                  pl.BlockSpec((1, LANES), lambda i: (0, 0))],
        out_specs=[pl.BlockSpec((rows, LANES), lambda i: (i, 0)),
                   pl.BlockSpec((rows, LANES), lambda i: (i, 0)),
                   pl.BlockSpec((rows // c, 8, c), lambda i: (i, 0, 0))],
        out_shape=[jax.ShapeDtypeStruct((m, LANES), F32),
                   jax.ShapeDtypeStruct((m, LANES), F32),
                   jax.ShapeDtypeStruct((m // c, 8, c), F32)],
        compiler_params=_params("parallel"),
        name="gdn_gates",
    )(hs, alog_row, dtb_row)


def _gdn_core_kernel(hq_ref, hk_ref, hv_ref, hz_ref, wq_ref, wk_ref, wv_ref, beta_ref, gcum_ref,
                     gcum_t_ref, ng_ref, o_ref, state_ref, xe_ref, *, rows, dk, hb):
    c = GDN_CHUNK
    n = rows // c
    group = pl.program_id(1)

    @pl.when(pl.program_id(2) == 0)
    def _():
        state_ref[...] = jnp.zeros_like(state_ref)
        xe_ref[:, :8, :] = jnp.zeros((3, 8, xe_ref.shape[2]), F32)

    def conv_silu(x_ref, w_ref, slot):
        w = w_ref[...]
        taps = w.shape[0]
        xe_ref[slot, 8:, :] = x_ref[...].astype(F32)
        y = w[taps - 1:taps, :] * xe_ref[slot, 8:, :]
        for t in range(taps - 1):
            y = y + w[t:t + 1, :] * xe_ref[slot, pl.ds(8 - taps + 1 + t, rows), :]
        xe_ref[slot, :8, :] = xe_ref[slot, rows:, :]
        return _silu(y)

    def l2norm(x):
        return x * lax.rsqrt(jnp.sum(x * x, axis=-1, keepdims=True) + RMS_EPS)

    k_all = conv_silu(hk_ref, wk_ref, 1)
    q_all = conv_silu(hq_ref, wq_ref, 0)
    v_all = conv_silu(hv_ref, wv_ref, 2)
    lane = _iota2((rows, LANES), 1)
    ri = _iota2((c, c), 0)
    ci = _iota2((c, c), 1)
    causal = ri >= ci
    strict = ri > ci
    eye = (ri == ci).astype(F32)
    dv = dk

    units = []
    for a in range(hb):
        head = group * hb + a
        cols = slice(a * dk, (a + 1) * dk)
        k = l2norm(k_all[:, cols])
        q = l2norm(q_all[:, cols]) * (dk ** -0.5)
        v = v_all[:, cols]
        beta = jnp.sum(jnp.where(lane == head, beta_ref[...], 0.0), axis=1, keepdims=True)
        gcol = jnp.sum(jnp.where(lane == head, gcum_ref[...], 0.0), axis=1, keepdims=True)
        for i in range(n):
            sl = slice(i * c, (i + 1) * c)
            units.append(dict(a=a, q=q[sl], k=k[sl], v=v[sl], beta=beta[sl], g=gcol[sl],
                              grow=gcum_t_ref[i, pl.ds(head, 1), :]))
    for u in units:
        u["glast"] = u["g"][c - 1:c, :]
        u["eg"] = jnp.exp(u["g"])
        u["decay"] = jnp.where(causal, jnp.exp(jnp.where(causal, u["g"] - u["grow"], 0.0)), 0.0)
        u["kb"] = u["k"] * u["beta"]
    for u in units:
        kbf = u["k"].astype(BF16)
        u["kk"] = lax.dot_general(u["kb"].astype(BF16), kbf, _NT, preferred_element_type=F32)
        u["qk"] = lax.dot_general(u["q"].astype(BF16), kbf, _NT, preferred_element_type=F32)
    for u in units:
        u["lmat"] = jnp.where(strict, u["kk"] * u["decay"], 0.0)
        u["amat"] = jnp.where(causal, u["qk"] * u["decay"], 0.0)
        u["tinv"] = eye - u["lmat"]
    for u in units:
        u["pk"] = _mm(u["lmat"], u["lmat"])
    span = 4
    while span < c:
        for u in units:
            u["both"] = _mm(u["pk"], jnp.concatenate([u["tinv"], u["pk"]], axis=1))
        for u in units:
            u["tinv"] = u["tinv"] + u["both"][:, :c]
            u["pk"] = u["both"][:, c:]
        span *= 2
    for u in units:
        u["tinv"] = u["tinv"] + _mm(u["pk"], u["tinv"])
    for u in units:
        u["x"] = _mm(u["tinv"], jnp.concatenate([u["v"] * u["beta"], u["kb"] * u["eg"]], axis=1))
    for u in units:
        u["ax"] = _mm(u["amat"], u["x"])
        kd = u["k"] * jnp.exp(u["glast"] - u["g"])
        u["kx"] = lax.dot_general(kd.astype(BF16), u["x"].astype(BF16), _TN, preferred_element_type=F32)
    for u in units:
        u["qeff"] = u["q"] * u["eg"] - u["ax"][:, dv:]
    states = [state_ref[a] for a in range(hb)]
    outs = [[] for _ in range(hb)]
    for i in range(n):
        for a in range(hb):
            u = units[a * n + i]
            sb = states[a].astype(BF16)
            outs[a].append(_mm(u["qeff"], sb) + u["ax"][:, :dv])
            states[a] = jnp.exp(u["glast"]) * states[a] + u["kx"][:, :dv] - _mm(u["kx"][:, dv:], sb)
    for a in range(hb):
        state_ref[a] = states[a]
        o = jnp.concatenate(outs[a], axis=0)
        o = o * lax.rsqrt(jnp.mean(o * o, axis=-1, keepdims=True) + RMS_EPS) * ng_ref[...]
        cols = slice(a * dk, (a + 1) * dk)
        o_ref[:, cols] = (o * _silu(hz_ref[:, cols].astype(F32))).astype(o_ref.dtype)


def _gdn_core(h_main, conv_w, beta, gcum, gcum_t, ng_row, *, batch, seq, heads, dk, rows, hb):
    m = h_main.shape[0]
    nb = seq // rows
    c = GDN_CHUNK
    taps = conv_w.shape[0]
    groups = heads // hb
    wide = hb * dk

    def col(off):
        return pl.BlockSpec((rows, wide), lambda b, h, j: (b * nb + j, off + h))

    def wcol(off):
        return pl.BlockSpec((taps, wide), lambda b, h, j: (0, off + h))

    def gate():
        return pl.BlockSpec((rows, LANES), lambda b, h, j: (b * nb + j, 0))

    return pl.pallas_call(
        functools.partial(_gdn_core_kernel, rows=rows, dk=dk, hb=hb),
        grid=(batch, groups, nb),
        in_specs=[col(0), col(groups), col(2 * groups), col(3 * groups),
                  wcol(0), wcol(groups), wcol(2 * groups),
                  gate(), gate(),
                  pl.BlockSpec((rows // c, 8, c), lambda b, h, j: (b * nb + j, 0, 0)),
                  pl.BlockSpec((1, dk), lambda b, h, j: (0, 0))],
        out_specs=pl.BlockSpec((rows, wide), lambda b, h, j: (b * nb + j, h)),
        out_shape=jax.ShapeDtypeStruct((m, heads * dk), BF16),
        scratch_shapes=[pltpu.VMEM((hb, dk, dk), F32), pltpu.VMEM((3, rows + 8, wide), F32)],
        compiler_params=_params("parallel", "parallel", "arbitrary"),
        name="gdn_core",
    )(h_main, h_main, h_main, h_main, conv_w, conv_w, conv_w, beta, gcum, gcum_t, ng_row)


FOX_VROWS = 80


def _split3(x):
    hi = x.astype(BF16).astype(F32)
    r = x - hi
    mid = r.astype(BF16).astype(F32)
    lo = (r - mid).astype(BF16).astype(F32)
    return hi, mid, lo


def _fox_prep_kernel(hq_ref, hk_ref, hv_ref, hf_ref, bf_ref, qg_ref, kg_ref, selq_ref, selk_ref,
                     qa_ref, ka_ref, vt_ref, carry_ref, *, rows, heads, dh):
    @pl.when(pl.program_id(1) == 0)
    def _():
        carry_ref[...] = jnp.zeros_like(carry_ref)

    logf = -_softplus(-(hf_ref[...] + bf_ref[...]))
    lane = _iota2((rows, LANES), 1)
    logf = jnp.where(lane < heads, logf, 0.0)
    tril = (_iota2((rows, rows), 0) >= _iota2((rows, rows), 1)).astype(F32)
    c = _mm_f32(tril, logf) + carry_ref[...]
    carry_ref[...] = c[rows - 1:rows, :]
    chi, cmid, clo = _split3(c * LOG2E)
    parts = (chi + pltpu.roll(cmid, heads, axis=1) + pltpu.roll(clo, 2 * heads, axis=1)
             + jnp.where(lane == 3 * heads, 1.0, 0.0)).astype(BF16)
    q_tails = jnp.dot(parts, selq_ref[...], preferred_element_type=F32)
    k_tails = jnp.dot(parts, selk_ref[...], preferred_element_type=F32)

    same_head = (_iota2((LANES, LANES), 0) // dh == _iota2((LANES, LANES), 1) // dh).astype(BF16)

    def head_rms(x, g, scale):
        ssum = jnp.dot((x * x).astype(BF16), same_head, preferred_element_type=F32)
        return x * lax.rsqrt(ssum * (1.0 / dh) + RMS_EPS) * (g * scale)

    per_block = LANES // dh
    vrow = _iota2((FOX_VROWS, LANES), 0)
    vlane = _iota2((FOX_VROWS, LANES), 1)
    out_row = _iota2((FOX_VROWS, rows), 0)
    for j in range(heads // per_block):
        cols = slice(j * LANES, (j + 1) * LANES)
        qn = head_rms(hq_ref[:, cols].astype(F32), qg_ref[:, cols], dh ** -0.5 * LOG2E)
        kn = head_rms(hk_ref[:, cols].astype(F32), kg_ref[:, cols], 1.0)
        vb = hv_ref[:, cols].astype(BF16)
        for a in range(per_block):
            h = j * per_block + a
            qh = qn if a == 0 else pltpu.roll(qn, LANES - a * dh, axis=1)
            kh = kn if a == 0 else pltpu.roll(kn, LANES - a * dh, axis=1)
            out_cols = slice(h * LANES, (h + 1) * LANES)
            qa_ref[:, out_cols] = jnp.where(lane < dh, qh, q_tails[:, out_cols]).astype(qa_ref.dtype)
            ka_ref[:, out_cols] = jnp.where(lane < dh, kh, k_tails[:, out_cols]).astype(ka_ref.dtype)
            pick = ((vlane == a * dh + vrow) & (vrow < dh)).astype(BF16)
            vt = lax.dot_general(pick, vb, _NT, preferred_element_type=F32)
            vt_ref[0, h] = jnp.where(out_row == dh, 1.0, vt).astype(vt_ref.dtype)


def _fox_bias_selectors(heads, dh):
    j = np.arange(LANES)[:, None]
    col = np.arange(heads * LANES)[None, :]
    h, l = col // LANES, col % LANES - dh
    part = lambda i: (l == i) & (j == i * heads + h)
    ones = lambda lo: (l >= lo) & (l < lo + 3) & (j == 3 * heads)
    sel_q = part(0) | part(1) | part(2) | ones(3)
    sel_k = ones(0).astype(np.float32) - ((l == 3) & (j == h)) - ((l == 4) & (j == heads + h)) \
        - ((l == 5) & (j == 2 * heads + h))
    return jnp.asarray(sel_q, BF16), jnp.asarray(sel_k, BF16)


def _fox_prep(h_main, hf, bf_row, qg_row, kg_row, *, batch, seq, heads, dh, rows):
    m = h_main.shape[0]
    w = heads * dh
    nt = seq // rows
    sel_q, sel_k = _fox_bias_selectors(heads, dh)

    def col(j):
        return pl.BlockSpec((rows, w), lambda b, t: (b * nt + t, j))

    def const(r, n):
        return pl.BlockSpec((r, n), lambda b, t: (0, 0))

    aug = pl.BlockSpec((rows, heads * LANES), lambda b, t: (b * nt + t, 0))
    return pl.pallas_call(
        functools.partial(_fox_prep_kernel, rows=rows, heads=heads, dh=dh),
        grid=(batch, nt),
        in_specs=[col(0), col(1), col(2),
                  pl.BlockSpec((rows, LANES), lambda b, t: (b * nt + t, 0)),
                  const(1, LANES), const(1, w), const(1, w),
                  const(LANES, heads * LANES), const(LANES, heads * LANES)],
        out_specs=[aug, aug,
                   pl.BlockSpec((1, heads, FOX_VROWS, rows), lambda b, t: (b, 0, 0, t))],
        out_shape=[jax.ShapeDtypeStruct((m, heads * LANES), BF16),
                   jax.ShapeDtypeStruct((m, heads * LANES), BF16),
                   jax.ShapeDtypeStruct((batch, heads, FOX_VROWS, seq), BF16)],
        scratch_shapes=[pltpu.VMEM((1, LANES), F32)],
        compiler_params=_params("parallel", "arbitrary"),
        name="fox_prep",
    )(h_main, h_main, h_main, hf, bf_row, qg_row, kg_row, sel_q, sel_k)


def _fox_attn_kernel(q_ref, k_ref, v_ref, z_ref, o_ref, s0_ref, s1_ref, m_ref, acc_ref, *,
                     tile, dh, per_block, nq, unroll):
    below_diag = _iota2((tile, tile), 0) <= _iota2((tile, tile), 1)

    def logits(qi, kj, s_ref):
        qs = pl.multiple_of(qi * tile, tile)
        ks = pl.multiple_of(kj * tile, tile)
        for a in range(per_block):
            s_ref[a] = lax.dot_general(k_ref[pl.ds(ks, tile), a * LANES:(a + 1) * LANES],
                                       q_ref[pl.ds(qs, tile), a * LANES:(a + 1) * LANES], _NT,
                                       preferred_element_type=F32)

    def softmax_pv(qi, kj, s_ref, diagonal):
        ks = pl.multiple_of(kj * tile, tile)
        for a in range(per_block):
            vt = v_ref[0, a, :, pl.ds(ks, tile)]
            if diagonal:
                s = jnp.where(below_diag, s_ref[a], -jnp.inf)
                m_new = jnp.max(s, axis=0, keepdims=True)
                p = jnp.exp2(s - m_new).astype(BF16)
                acc_ref[qi, a] = jnp.dot(vt, p, preferred_element_type=F32)
            else:
                s = s_ref[a]
                m_prev = m_ref[qi, a]
                m_new = jnp.maximum(m_prev, jnp.max(s, axis=0, keepdims=True))
                p = jnp.exp2(s - m_new).astype(BF16)
                acc_ref[qi, a] = (jnp.exp2(m_prev - m_new) * acc_ref[qi, a]
                                  + jnp.dot(vt, p, preferred_element_type=F32))
            m_ref[qi, a] = m_new

    bufs = (s0_ref, s1_ref)

    logits(0, 0, s0_ref)

    def diagonal_steps(u, _):
        for r in range(unroll):
            d = unroll * u + r
            nxt = jnp.minimum(d + 1, nq - 1)
            logits(nxt, nxt, bufs[(r + 1) % 2])
            softmax_pv(d, d, bufs[r % 2], True)
        return 0

    lax.fori_loop(0, nq // unroll, diagonal_steps, 0)

    def following(qi, kj):
        wrap = kj + 1 >= qi
        last = jnp.logical_and(wrap, qi >= nq - 1)
        return (jnp.where(last, qi, jnp.where(wrap, qi + 1, qi)),
                jnp.where(last, kj, jnp.where(wrap, 0, kj + 1)))

    logits(1, 0, s0_ref)

    def lower_steps(u, task):
        for r in range(unroll):
            nxt = following(*task)
            logits(nxt[0], nxt[1], bufs[(r + 1) % 2])
            softmax_pv(task[0], task[1], bufs[r % 2], False)
            task = nxt
        return task

    lax.fori_loop(0, nq * (nq - 1) // 2 // unroll, lower_steps, (jnp.int32(1), jnp.int32(0)))

    def finish(qi, _):
        qs = pl.multiple_of(qi * tile, tile)
        o_t = jnp.concatenate([acc_ref[qi, a, :dh, :] / acc_ref[qi, a, dh:dh + 1, :] for a in range(per_block)],
                              axis=0)
        o_ref[pl.ds(qs, tile), :] = (o_t.T * _silu(z_ref[pl.ds(qs, tile), :].astype(F32))).astype(o_ref.dtype)
        return 0

    lax.fori_loop(0, nq, finish, 0)


def _fox_attn(qa, ka, vt, h_main, *, batch, seq, heads, dh, tile, unroll):
    m = qa.shape[0]
    nq = seq // tile
    assert unroll % 2 == 0 and nq % unroll == 0 and (nq * (nq - 1) // 2) % unroll == 0
    per_block = LANES // dh
    blocks = heads // per_block
    z_off = 3 * blocks
    wide = per_block * LANES
    return pl.pallas_call(
        functools.partial(_fox_attn_kernel, tile=tile, dh=dh, per_block=per_block, nq=nq, unroll=unroll),
        grid=(batch, blocks),
        in_specs=[pl.BlockSpec((seq, wide), lambda b, h: (b, h)),
                  pl.BlockSpec((seq, wide), lambda b, h: (b, h)),
                  pl.BlockSpec((1, per_block, FOX_VROWS, seq), lambda b, h: (b, h, 0, 0)),
                  pl.BlockSpec((seq, LANES), lambda b, h: (b, z_off + h))],
        out_specs=pl.BlockSpec((seq, LANES), lambda b, h: (b, h)),
        out_shape=jax.ShapeDtypeStruct((m, heads * dh), BF16),
        scratch_shapes=[pltpu.VMEM((per_block, tile, tile), F32), pltpu.VMEM((per_block, tile, tile), F32),
                        pltpu.VMEM((nq, per_block, 1, tile), F32),
                        pltpu.VMEM((nq, per_block, FOX_VROWS, tile), F32)],
        compiler_params=_params("parallel", "parallel"),
        name="fox_attn",
    )(qa, ka, vt, h_main)


def _post_kernel(a_ref, x_ref, p_ref, wo_ref, g_ref, b_ref, wg_ref, wp_ref, o_ref, *, alpha):
    t = alpha * x_ref[...] + jnp.dot(a_ref[...], wo_ref[...], preferred_element_type=F32)
    mu = jnp.mean(t, axis=-1, keepdims=True)
    d = t - mu
    var = jnp.mean(d * d, axis=-1, keepdims=True)
    xn = d * lax.rsqrt(var + LN_EPS) * g_ref[...] + b_ref[...]
    gate = _sigmoid(_mm(xn, wg_ref[...]))
    o_ref[...] = xn + gate * _mm(p_ref[...], wp_ref[...])


def _post(a, x, p, w_out, ln_g, ln_b, w_gate, w_proj, *, alpha, tm):
    m, d = x.shape
    dp = p.shape[1]

    def rows(n):
        return pl.BlockSpec((tm, n), lambda i: (i, 0))

    def full(r, n):
        return pl.BlockSpec((r, n), lambda i: (0, 0))

    return pl.pallas_call(
        functools.partial(_post_kernel, alpha=alpha),
        grid=(m // tm,),
        in_specs=[rows(a.shape[1]), rows(d), rows(dp), full(a.shape[1], d), full(1, d), full(1, d),
                  full(d, d), full(dp, d)],
        out_specs=rows(d),
        out_shape=jax.ShapeDtypeStruct((m, d), F32),
        compiler_params=_params("parallel"),
        name="post_block",
    )(a, x, p, w_out, ln_g, ln_b, w_gate, w_proj)


def _pad_cols(w, n):
    return jnp.pad(w, ((0, 0), (0, n - w.shape[1])))


def _pad_row(v, n):
    return jnp.pad(v, (0, n - v.shape[0])).reshape(1, n)


def _gdn_layer(x, w_in, conv_w, a_log, dt_bias, norm_g, *, batch, seq):
    heads = a_log.shape[0]
    dk = norm_g.shape[0]
    wide = 4 * heads * dk
    w_gates = jnp.concatenate([_pad_cols(w_in[:, wide:wide + heads], LANES),
                               _pad_cols(w_in[:, wide + heads:], LANES)], axis=1).astype(BF16)
    h_main, hs = _in_proj(x, w_in[:, :wide].astype(BF16), w_gates, tm=512, tn=1024, name="gdn_in_proj")
    beta, gcum, gcum_t = _gdn_gates(hs, _pad_row(a_log, LANES), _pad_row(dt_bias, LANES), rows=512)
    return _gdn_core(h_main, conv_w, beta, gcum, gcum_t, norm_g.reshape(1, dk),
                     batch=batch, seq=seq, heads=heads, dk=dk, rows=512, hb=4)


def _fox_layer(x, w_in, b_f, q_norm_g, k_norm_g, *, batch, seq):
    heads = b_f.shape[0]
    dh = q_norm_g.shape[0]
    wide = 4 * heads * dh
    h_main, hf = _in_proj(x, w_in[:, :wide].astype(BF16), _pad_cols(w_in[:, wide:], LANES).astype(BF16),
                          tm=512, tn=1024, name="fox_in_proj")
    qa, ka, vt = _fox_prep(h_main, hf, _pad_row(b_f, LANES),
                           jnp.tile(q_norm_g, heads).reshape(1, heads * dh),
                           jnp.tile(k_norm_g, heads).reshape(1, heads * dh),
                           batch=batch, seq=seq, heads=heads, dh=dh, rows=512)
    return _fox_attn(qa, ka, vt, h_main, batch=batch, seq=seq, heads=heads, dh=dh, tile=512, unroll=8)


def kernel(x, p, ln_g, ln_b, ple_w_gate, ple_w_proj, gdn_w_in, gdn_conv_w, gdn_a_log, gdn_dt_bias,
           gdn_norm_g, gdn_w_out, fox_w_in, fox_b_f, fox_q_norm_g, fox_k_norm_g, fox_w_out):
    batch, seq, d = x.shape
    depth = ln_g.shape[0]
    alpha = (2 * depth) ** 0.25
    xf = x.reshape(batch * seq, d)
    for i in range(depth):
        j = i // 2
        if i % 2 == 0:
            a = _gdn_layer(xf, gdn_w_in[j], gdn_conv_w[j], gdn_a_log[j], gdn_dt_bias[j], gdn_norm_g[j],
                           batch=batch, seq=seq)
            w_out = gdn_w_out[j]
        else:
            a = _fox_layer(xf, fox_w_in[j], fox_b_f[j], fox_q_norm_g[j], fox_k_norm_g[j],
                           batch=batch, seq=seq)
            w_out = fox_w_out[j]
        xf = _post(a, xf, p[i].reshape(batch * seq, -1), w_out.astype(BF16), ln_g[i].reshape(1, d),
                   ln_b[i].reshape(1, d), ple_w_gate[i].astype(BF16), ple_w_proj[i].astype(BF16),
                   alpha=alpha, tm=512)
    return xf.reshape(batch, seq, d)
```

```python
import functools

import jax
import jax.numpy as jnp
import numpy as np
from jax import lax
from jax.experimental import pallas as pl
from jax.experimental.pallas import tpu as pltpu

F32 = jnp.float32
BF16 = jnp.bfloat16

LANES = 128
V7X_VMEM_BYTES = 64 * 1024 * 1024
VMEM_LIMIT = V7X_VMEM_BYTES * 3 // 4

GDN_CHUNK = 64
LN_EPS = 1e-5
RMS_EPS = 1e-6
LOG2E = 1.4426950408889634

_NT = (((1,), (1,)), ((), ()))
_TN = (((0,), (0,)), ((), ()))


def _mm(a, b):
    return jnp.dot(a.astype(BF16), b.astype(BF16), preferred_element_type=F32)


def _sigmoid(x):
    return 1.0 / (1.0 + jnp.exp(-x))


def _silu(x):
    return x * _sigmoid(x)


def _softplus(x):
    return jnp.maximum(x, 0.0) + jnp.log1p(jnp.exp(-jnp.abs(x)))


def _split3(x):
    hi = x.astype(BF16).astype(F32)
    r = x - hi
    mid = r.astype(BF16).astype(F32)
    lo = (r - mid).astype(BF16).astype(F32)
    return hi, mid, lo


def _iota2(shape, dim):
    return lax.broadcasted_iota(jnp.int32, shape, dim)


def _params(*semantics):
    return pltpu.CompilerParams(dimension_semantics=semantics, vmem_limit_bytes=VMEM_LIMIT)


def _in_proj_kernel(x_ref, wm_ref, ws_ref, hm_ref, hs_ref, *, tn):
    xb = x_ref[...].astype(BF16)
    for j in range(wm_ref.shape[1] // tn):
        cols = slice(j * tn, (j + 1) * tn)
        hm_ref[:, cols] = jnp.dot(xb, wm_ref[:, cols], preferred_element_type=F32).astype(hm_ref.dtype)
    hs_ref[...] = jnp.dot(xb, ws_ref[...], preferred_element_type=F32)


def _in_proj(x, w_main, w_small, *, tm, tn, name):
    m, k = x.shape
    n = w_main.shape[1]
    ns = w_small.shape[1]
    return pl.pallas_call(
        functools.partial(_in_proj_kernel, tn=tn),
        grid=(m // tm,),
        in_specs=[pl.BlockSpec((tm, k), lambda i: (i, 0)),
                  pl.BlockSpec((k, n), lambda i: (0, 0)),
                  pl.BlockSpec((k, ns), lambda i: (0, 0))],
        out_specs=[pl.BlockSpec((tm, n), lambda i: (i, 0)),
                   pl.BlockSpec((tm, ns), lambda i: (i, 0))],
        out_shape=[jax.ShapeDtypeStruct((m, n), BF16), jax.ShapeDtypeStruct((m, ns), F32)],
        compiler_params=_params("parallel"),
        name=name,
    )(x, w_main, w_small)


def _gdn_gate_kernel(hs_ref, alog_ref, dtb_ref, beta_ref, gcum_ref, gcum_t_ref, *, rows):
    c = GDN_CHUNK
    beta_ref[...] = _sigmoid(hs_ref[:, :LANES])
    g = -jnp.exp(alog_ref[...]) * _softplus(hs_ref[:, LANES:] + dtb_ref[...])
    g3 = jnp.concatenate([part.astype(BF16) for part in _split3(g)], axis=1)
    tril = (_iota2((c, c), 0) >= _iota2((c, c), 1)).astype(BF16)
    eye = (_iota2((8, LANES), 0) == _iota2((8, LANES), 1)).astype(BF16)
    for i in range(rows // c):
        sums = jnp.dot(tril, g3[i * c:(i + 1) * c], preferred_element_type=F32)
        gc = sums[:, :LANES] + sums[:, LANES:2 * LANES] + sums[:, 2 * LANES:]
        gcum_ref[i * c:(i + 1) * c, :] = gc
        hi, mid, lo = (lax.dot_general(eye, part.astype(BF16), _NT, preferred_element_type=F32)
                       for part in _split3(gc))
        gcum_t_ref[i] = hi + mid + lo


def _gdn_gates(hs, alog_row, dtb_row, *, rows):
    m = hs.shape[0]
    c = GDN_CHUNK
    return pl.pallas_call(
        functools.partial(_gdn_gate_kernel, rows=rows),
        grid=(m // rows,),
        in_specs=[pl.BlockSpec((rows, 2 * LANES), lambda i: (i, 0)),
                  pl.BlockSpec((1, LANES), lambda i: (0, 0)),
                  pl.BlockSpec((1, LANES), lambda i: (0, 0))],
        out_specs=[pl.BlockSpec((rows, LANES), lambda i: (i, 0)),
                   pl.BlockSpec((rows, LANES), lambda i: (i, 0)),
                   pl.BlockSpec((rows // c, 8, c), lambda i: (i, 0, 0))],
        out_shape=[jax.ShapeDtypeStruct((m, LANES), F32),
                   jax.ShapeDtypeStruct((m, LANES), F32),
                   jax.ShapeDtypeStruct((m // c, 8, c), F32)],
        compiler_params=_params("parallel"),
        name="gdn_gates",
    )(hs, alog_row, dtb_row)


def _gdn_core_kernel(hq_ref, hk_ref, hv_ref, hz_ref, wq_ref, wk_ref, wv_ref, beta_ref, gcum_ref,
                     gcum_t_ref, ng_ref, o_ref, state_ref, xe_ref, *, rows, dk, hb):
    c = GDN_CHUNK
    n = rows // c
    group = pl.program_id(1)

    @pl.when(pl.program_id(2) == 0)
    def _():
        state_ref[...] = jnp.zeros_like(state_ref)
        xe_ref[:, :8, :] = jnp.zeros((3, 8, xe_ref.shape[2]), F32)

    def conv_silu(x_ref, w_ref, slot):
        w = w_ref[...]
        taps = w.shape[0]
        xe_ref[slot, 8:, :] = x_ref[...].astype(F32)
        y = w[taps - 1:taps, :] * xe_ref[slot, 8:, :]
        for t in range(taps - 1):
            y = y + w[t:t + 1, :] * xe_ref[slot, pl.ds(8 - taps + 1 + t, rows), :]
        xe_ref[slot, :8, :] = xe_ref[slot, rows:, :]
        return _silu(y)

    def l2norm(x):
        return x * lax.rsqrt(jnp.sum(x * x, axis=-1, keepdims=True) + RMS_EPS)

    k_all = conv_silu(hk_ref, wk_ref, 1)
    q_all = conv_silu(hq_ref, wq_ref, 0)
    v_all = conv_silu(hv_ref, wv_ref, 2)
    lane = _iota2((rows, LANES), 1)
    ri = _iota2((c, c), 0)
    ci = _iota2((c, c), 1)
    causal = ri >= ci
    strict = ri > ci
    eye = (ri == ci).astype(F32)
    dv = dk

    units = []
    for a in range(hb):
        head = group * hb + a
        cols = slice(a * dk, (a + 1) * dk)
        k = l2norm(k_all[:, cols])
        q = l2norm(q_all[:, cols]) * (dk ** -0.5)
        v = v_all[:, cols]
        beta = jnp.sum(jnp.where(lane == head, beta_ref[...], 0.0), axis=1, keepdims=True)
        gcol = jnp.sum(jnp.where(lane == head, gcum_ref[...], 0.0), axis=1, keepdims=True)
        for i in range(n):
            sl = slice(i * c, (i + 1) * c)
            units.append(dict(a=a, q=q[sl], k=k[sl], v=v[sl], beta=beta[sl], g=gcol[sl],
                              grow=gcum_t_ref[i, pl.ds(head, 1), :]))
    for u in units:
        u["glast"] = u["g"][c - 1:c, :]
        u["eg"] = jnp.exp(u["g"])
        u["decay"] = jnp.where(causal, jnp.exp(jnp.where(causal, u["g"] - u["grow"], 0.0)), 0.0)
        u["kb"] = u["k"] * u["beta"]
    for u in units:
        kbf = u["k"].astype(BF16)
        u["kk"] = lax.dot_general(u["kb"].astype(BF16), kbf, _NT, preferred_element_type=F32)
        u["qk"] = lax.dot_general(u["q"].astype(BF16), kbf, _NT, preferred_element_type=F32)
    for u in units:
        u["lmat"] = jnp.where(strict, u["kk"] * u["decay"], 0.0)
        u["amat"] = jnp.where(causal, u["qk"] * u["decay"], 0.0)
        u["tinv"] = eye - u["lmat"]
    for u in units:
        u["pk"] = _mm(u["lmat"], u["lmat"])
    span = 4
    while span < c:
        for u in units:
            u["both"] = _mm(u["pk"], jnp.concatenate([u["tinv"], u["pk"]], axis=1))
        for u in units:
            u["tinv"] = u["tinv"] + u["both"][:, :c]
            u["pk"] = u["both"][:, c:]
        span *= 2
    for u in units:
        u["tinv"] = u["tinv"] + _mm(u["pk"], u["tinv"])
    for u in units:
        u["x"] = _mm(u["tinv"], jnp.concatenate([u["v"] * u["beta"], u["kb"] * u["eg"]], axis=1))
    for u in units:
        u["ax"] = _mm(u["amat"], u["x"])
        kd = u["k"] * jnp.exp(u["glast"] - u["g"])
        u["kx"] = lax.dot_general(kd.astype(BF16), u["x"].astype(BF16), _TN, preferred_element_type=F32)
    for u in units:
        u["qeff"] = u["q"] * u["eg"] - u["ax"][:, dv:]
    states = [state_ref[a] for a in range(hb)]
    outs = [[] for _ in range(hb)]
    for i in range(n):
        for a in range(hb):
            u = units[a * n + i]
            sb = states[a].astype(BF16)
            outs[a].append(_mm(u["qeff"], sb) + u["ax"][:, :dv])
            states[a] = jnp.exp(u["glast"]) * states[a] + u["kx"][:, :dv] - _mm(u["kx"][:, dv:], sb)
    for a in range(hb):
        state_ref[a] = states[a]
        o = jnp.concatenate(outs[a], axis=0)
        o = o * lax.rsqrt(jnp.mean(o * o, axis=-1, keepdims=True) + RMS_EPS) * ng_ref[...]
        cols = slice(a * dk, (a + 1) * dk)
        o_ref[:, cols] = (o * _silu(hz_ref[:, cols].astype(F32))).astype(o_ref.dtype)


def _gdn_core(h_main, conv_w, beta, gcum, gcum_t, ng_row, *, batch, seq, heads, dk, rows, hb):
    m = h_main.shape[0]
    nb = seq // rows
    c = GDN_CHUNK
    taps = conv_w.shape[0]
    groups = heads // hb
    wide = hb * dk

    def col(off):
        return pl.BlockSpec((rows, wide), lambda b, h, j: (b * nb + j, off + h))

    def wcol(off):
        return pl.BlockSpec((taps, wide), lambda b, h, j: (0, off + h))

    def gate():
        return pl.BlockSpec((rows, LANES), lambda b, h, j: (b * nb + j, 0))

    return pl.pallas_call(
        functools.partial(_gdn_core_kernel, rows=rows, dk=dk, hb=hb),
        grid=(batch, groups, nb),
        in_specs=[col(0), col(groups), col(2 * groups), col(3 * groups),
                  wcol(0), wcol(groups), wcol(2 * groups),
                  gate(), gate(),
                  pl.BlockSpec((rows // c, 8, c), lambda b, h, j: (b * nb + j, 0, 0)),
                  pl.BlockSpec((1, dk), lambda b, h, j: (0, 0))],
        out_specs=pl.BlockSpec((rows, wide), lambda b, h, j: (b * nb + j, h)),
        out_shape=jax.ShapeDtypeStruct((m, heads * dk), BF16),
        scratch_shapes=[pltpu.VMEM((hb, dk, dk), F32), pltpu.VMEM((3, rows + 8, wide), F32)],
        compiler_params=_params("parallel", "parallel", "arbitrary"),
        name="gdn_core",
    )(h_main, h_main, h_main, h_main, conv_w, conv_w, conv_w, beta, gcum, gcum_t, ng_row)


FOX_TILE = 512
FOX_VROWS = 80


def _fox_prep_kernel(hq_ref, hk_ref, hv_ref, hf_ref, bf_ref, qg_ref, kg_ref, selq_ref, selk_ref,
                     qa_ref, ka_ref, vt_ref, edge_ref, carry_ref, *, rows, heads, dh):
    @pl.when(pl.program_id(1) == 0)
    def _():
        carry_ref[...] = jnp.zeros_like(carry_ref)

    logf = -_softplus(-(hf_ref[...] + bf_ref[...]))
    lane = _iota2((rows, LANES), 1)
    logf = jnp.where(lane < heads, logf, 0.0)
    tril = (_iota2((rows, rows), 0) >= _iota2((rows, rows), 1)).astype(BF16)
    sums = jnp.dot(tril, jnp.concatenate([part.astype(BF16) for part in _split3(logf)], axis=1),
                   preferred_element_type=F32)
    c = sums[:, :LANES] + sums[:, LANES:2 * LANES] + sums[:, 2 * LANES:] + carry_ref[...]
    carry_ref[...] = c[rows - 1:rows, :]
    c2 = c * LOG2E
    edge_ref[0] = jnp.concatenate([c2[:1], c2[rows - 1:], jnp.zeros((6, LANES), F32)], axis=0)
    chi, cmid, clo = _split3(c2)
    parts = (chi + pltpu.roll(cmid, heads, axis=1) + pltpu.roll(clo, 2 * heads, axis=1)
             + jnp.where(lane == 3 * heads, 1.0, 0.0)).astype(BF16)
    q_tails = jnp.dot(parts, selq_ref[...], preferred_element_type=F32)
    k_tails = jnp.dot(parts, selk_ref[...], preferred_element_type=F32)

    same_head = (_iota2((LANES, LANES), 0) // dh == _iota2((LANES, LANES), 1) // dh).astype(BF16)

    def head_rms(x, g, scale):
        ssum = jnp.dot((x * x).astype(BF16), same_head, preferred_element_type=F32)
        return x * lax.rsqrt(ssum * (1.0 / dh) + RMS_EPS) * (g * scale)

    per_block = LANES // dh
    vrow = _iota2((FOX_VROWS, LANES), 0)
    vlane = _iota2((FOX_VROWS, LANES), 1)
    out_row = _iota2((FOX_VROWS, rows), 0)
    for j in range(heads // per_block):
        cols = slice(j * LANES, (j + 1) * LANES)
        qn = head_rms(hq_ref[:, cols].astype(F32), qg_ref[:, cols], dh ** -0.5 * LOG2E)
        kn = head_rms(hk_ref[:, cols].astype(F32), kg_ref[:, cols], 1.0)
        vb = hv_ref[:, cols].astype(BF16)
        for a in range(per_block):
            h = j * per_block + a
            qh = qn if a == 0 else pltpu.roll(qn, LANES - a * dh, axis=1)
            kh = kn if a == 0 else pltpu.roll(kn, LANES - a * dh, axis=1)
            out_cols = slice(h * LANES, (h + 1) * LANES)
            qa_ref[:, out_cols] = jnp.where(lane < dh, qh, q_tails[:, out_cols]).astype(qa_ref.dtype)
            ka_ref[:, out_cols] = jnp.where(lane < dh, kh, k_tails[:, out_cols]).astype(ka_ref.dtype)
            pick = ((vlane == a * dh + vrow) & (vrow < dh)).astype(BF16)
            vt = lax.dot_general(pick, vb, _NT, preferred_element_type=F32)
            vt_ref[0, h] = jnp.where(out_row == dh, 1.0, vt).astype(vt_ref.dtype)


def _fox_bias_selectors(heads, dh):
    j = np.arange(LANES)[:, None]
    col = np.arange(heads * LANES)[None, :]
    h, l = col // LANES, col % LANES - dh
    part = lambda i: (l == i) & (j == i * heads + h)
    ones = lambda lo: (l >= lo) & (l < lo + 3) & (j == 3 * heads)
    sel_q = part(0) | part(1) | part(2) | ones(3)
    sel_k = ones(0).astype(np.float32) - ((l == 3) & (j == h)) - ((l == 4) & (j == heads + h)) \
        - ((l == 5) & (j == 2 * heads + h))
    return jnp.asarray(sel_q, BF16), jnp.asarray(sel_k, BF16)


def _fox_prep(h_main, hf, bf_row, qg_row, kg_row, *, batch, seq, heads, dh, rows):
    m = h_main.shape[0]
    w = heads * dh
    nt = seq // rows
    sel_q, sel_k = _fox_bias_selectors(heads, dh)

    def col(j):
        return pl.BlockSpec((rows, w), lambda b, t: (b * nt + t, j))

    def const(r, n):
        return pl.BlockSpec((r, n), lambda b, t: (0, 0))

    aug = pl.BlockSpec((rows, heads * LANES), lambda b, t: (b * nt + t, 0))
    return pl.pallas_call(
        functools.partial(_fox_prep_kernel, rows=rows, heads=heads, dh=dh),
        grid=(batch, nt),
        in_specs=[col(0), col(1), col(2),
                  pl.BlockSpec((rows, LANES), lambda b, t: (b * nt + t, 0)),
                  const(1, LANES), const(1, w), const(1, w),
                  const(LANES, heads * LANES), const(LANES, heads * LANES)],
        out_specs=[aug, aug,
                   pl.BlockSpec((1, heads, FOX_VROWS, rows), lambda b, t: (b, 0, 0, t)),
                   pl.BlockSpec((1, 8, LANES), lambda b, t: (b * nt + t, 0, 0))],
        out_shape=[jax.ShapeDtypeStruct((m, heads * LANES), BF16),
                   jax.ShapeDtypeStruct((m, heads * LANES), BF16),
                   jax.ShapeDtypeStruct((batch, heads, FOX_VROWS, seq), BF16),
                   jax.ShapeDtypeStruct((batch * nt, 8, LANES), F32)],
        scratch_shapes=[pltpu.VMEM((1, LANES), F32)],
        compiler_params=_params("parallel", "arbitrary"),
        name="fox_prep",
    )(h_main, h_main, h_main, hf, bf_row, qg_row, kg_row, sel_q, sel_k)


def _fox_attn_kernel(tq_ref, tk_ref, steps_ref, q_ref, k_ref, v_ref, z_ref, o_ref, s0_ref, s1_ref, m_ref, acc_ref,
                     *, tile, dh, per_block, nq, unroll, heads, tasks):
    below_diag = _iota2((tile, tile), 0) <= _iota2((tile, tile), 1)

    def logits(a, qi, kj, s_ref):
        qs = pl.multiple_of(jnp.minimum(qi, nq - 1) * tile, tile)
        ks = pl.multiple_of(kj * tile, tile)
        s_ref[a] = lax.dot_general(k_ref[pl.ds(ks, tile), a * LANES:(a + 1) * LANES],
                                   q_ref[pl.ds(qs, tile), a * LANES:(a + 1) * LANES], _NT,
                                   preferred_element_type=F32)

    def softmax_pv(a, qi, kj, s_ref, diagonal):
        ks = pl.multiple_of(kj * tile, tile)
        vt = v_ref[0, a, :, pl.ds(ks, tile)]
        if diagonal:
            s = jnp.where(below_diag, s_ref[a], -jnp.inf)
            m_new = jnp.max(s, axis=0, keepdims=True)
            p = jnp.exp2(s - m_new).astype(BF16)
            acc_ref[qi, a] = jnp.dot(vt, p, preferred_element_type=F32)
        else:
            s = s_ref[a]
            m_prev = m_ref[qi, a]
            m_new = jnp.maximum(m_prev, jnp.max(s, axis=0, keepdims=True))
            p = jnp.exp2(s - m_new).astype(BF16)
            acc_ref[qi, a] = (jnp.exp2(m_prev - m_new) * acc_ref[qi, a]
                              + jnp.dot(vt, p, preferred_element_type=F32))
        m_ref[qi, a] = m_new

    bufs = (s0_ref, s1_ref)
    everyone = range(per_block)
    m_ref[nq] = jnp.zeros(m_ref.shape[1:], F32)
    acc_ref[nq] = jnp.zeros(acc_ref.shape[1:], F32)

    for a in everyone:
        logits(a, 0, 0, s0_ref)

    def diagonal_steps(u, _):
        for r in range(unroll):
            d = unroll * u + r
            nxt = jnp.minimum(d + 1, nq - 1)
            for a in everyone:
                logits(a, nxt, nxt, bufs[(r + 1) % 2])
            for a in everyone:
                softmax_pv(a, d, d, bufs[r % 2], True)
        return 0

    lax.fori_loop(0, nq // unroll, diagonal_steps, 0)

    for a in everyone:
        head = pl.program_id(1) * per_block + a
        base = (pl.program_id(0) * heads + head) * tasks
        logits(a, tq_ref[base], tk_ref[base], s0_ref)

        def lower_steps(u, _, a=a, base=base):
            for r in range(unroll):
                t = base + unroll * u + r
                nxt = jnp.minimum(t + 1, base + tasks - 1)
                logits(a, tq_ref[nxt], tk_ref[nxt], bufs[(r + 1) % 2])
                softmax_pv(a, tq_ref[t], tk_ref[t], bufs[r % 2], False)
            return 0

        lax.fori_loop(0, steps_ref[pl.program_id(0) * heads + head], lower_steps, 0)

    def finish(qi, _):
        qs = pl.multiple_of(qi * tile, tile)
        o_t = jnp.concatenate([acc_ref[qi, a, :dh, :] / acc_ref[qi, a, dh:dh + 1, :] for a in everyone],
                              axis=0)
        o_ref[pl.ds(qs, tile), :] = (o_t.T * _silu(z_ref[pl.ds(qs, tile), :].astype(F32))).astype(o_ref.dtype)
        return 0

    lax.fori_loop(0, nq, finish, 0)


FOX_UNDERFLOW_LOG2 = 150.0
FOX_NORM_SLACK = 1.02
FOX_BIAS_SLACK = 2.0


def _fox_task_lists(edges, q_gain, k_gain, *, batch, heads, dh, nq, unroll):
    qk_bound = FOX_NORM_SLACK * dh ** 0.5 * LOG2E * jnp.max(jnp.abs(q_gain)) * jnp.max(jnp.abs(k_gain))
    threshold = 2.0 * qk_bound + FOX_UNDERFLOW_LOG2 + FOX_BIAS_SLACK
    edges = edges.reshape(batch, nq, 8, LANES)
    first = edges[:, :, 0, :heads].transpose(0, 2, 1)
    last = edges[:, :, 1, :heads].transpose(0, 2, 1)
    qi = jnp.arange(nq)[:, None]
    kj = jnp.arange(nq)[None, :]
    keep = (kj < qi) & (first[..., :, None] - last[..., None, :] >= -threshold)
    keep = keep.reshape(batch, heads, nq * nq)
    tasks = nq * (nq - 1) // 2
    order = jnp.argsort(jnp.logical_not(keep), axis=-1, stable=True)[..., :tasks]
    count = jnp.sum(keep, axis=-1)
    real = jnp.arange(tasks) < count[..., None]
    tq = jnp.where(real, order // nq, nq).astype(jnp.int32)
    tk = jnp.where(real, order % nq, 0).astype(jnp.int32)
    steps = ((count + unroll - 1) // unroll).astype(jnp.int32)
    return tq.reshape(-1), tk.reshape(-1), steps.reshape(-1), tasks


def _fox_attn(qa, ka, vt, h_main, edges, q_gain, k_gain, *, batch, seq, heads, dh, tile, unroll):
    m = qa.shape[0]
    nq = seq // tile
    assert unroll % 2 == 0 and nq % unroll == 0 and (nq * (nq - 1) // 2) % unroll == 0
    per_block = LANES // dh
    blocks = heads // per_block
    z_off = 3 * blocks
    wide = per_block * LANES
    tq, tk, steps, tasks = _fox_task_lists(edges, q_gain, k_gain, batch=batch, heads=heads, dh=dh, nq=nq,
                                           unroll=unroll)
    grid_spec = pltpu.PrefetchScalarGridSpec(
        num_scalar_prefetch=3,
        grid=(batch, blocks),
        in_specs=[pl.BlockSpec((seq, wide), lambda b, h, *_: (b, h)),
                  pl.BlockSpec((seq, wide), lambda b, h, *_: (b, h)),
                  pl.BlockSpec((1, per_block, FOX_VROWS, seq), lambda b, h, *_: (b, h, 0, 0)),
                  pl.BlockSpec((seq, LANES), lambda b, h, *_: (b, z_off + h))],
        out_specs=pl.BlockSpec((seq, LANES), lambda b, h, *_: (b, h)),
        scratch_shapes=[pltpu.VMEM((per_block, tile, tile), F32), pltpu.VMEM((per_block, tile, tile), F32),
                        pltpu.VMEM((nq + 1, per_block, 1, tile), F32),
                        pltpu.VMEM((nq + 1, per_block, FOX_VROWS, tile), F32)])
    return pl.pallas_call(
        functools.partial(_fox_attn_kernel, tile=tile, dh=dh, per_block=per_block, nq=nq, unroll=unroll,
                          heads=heads, tasks=tasks),
        grid_spec=grid_spec,
        out_shape=jax.ShapeDtypeStruct((m, heads * dh), BF16),
        compiler_params=_params("parallel", "parallel"),
        name="fox_attn",
    )(tq, tk, steps, qa, ka, vt, h_main)


def _post_kernel(a_ref, x_ref, p_ref, wo_ref, g_ref, b_ref, wg_ref, wp_ref, o_ref, *, alpha):
    t = alpha * x_ref[...] + jnp.dot(a_ref[...], wo_ref[...], preferred_element_type=F32)
    mu = jnp.mean(t, axis=-1, keepdims=True)
    d = t - mu
    var = jnp.mean(d * d, axis=-1, keepdims=True)
    xn = d * lax.rsqrt(var + LN_EPS) * g_ref[...] + b_ref[...]
    gate = _sigmoid(_mm(xn, wg_ref[...]))
    o_ref[...] = xn + gate * _mm(p_ref[...], wp_ref[...])


def _post(a, x, p, w_out, ln_g, ln_b, w_gate, w_proj, *, alpha, tm):
    m, d = x.shape
    dp = p.shape[1]

    def rows(n):
        return pl.BlockSpec((tm, n), lambda i: (i, 0))

    def full(r, n):
        return pl.BlockSpec((r, n), lambda i: (0, 0))

    return pl.pallas_call(
        functools.partial(_post_kernel, alpha=alpha),
        grid=(m // tm,),
        in_specs=[rows(a.shape[1]), rows(d), rows(dp), full(a.shape[1], d), full(1, d), full(1, d),
                  full(d, d), full(dp, d)],
        out_specs=rows(d),
        out_shape=jax.ShapeDtypeStruct((m, d), F32),
        compiler_params=_params("parallel"),
        name="post_block",
    )(a, x, p, w_out, ln_g, ln_b, w_gate, w_proj)


def _pad_cols(w, n):
    return jnp.pad(w, ((0, 0), (0, n - w.shape[1])))


def _pad_row(v, n):
    return jnp.pad(v, (0, n - v.shape[0])).reshape(1, n)


def _gdn_layer(x, w_in, conv_w, a_log, dt_bias, norm_g, *, batch, seq):
    heads = a_log.shape[0]
    dk = norm_g.shape[0]
    wide = 4 * heads * dk
    w_gates = jnp.concatenate([_pad_cols(w_in[:, wide:wide + heads], LANES),
                               _pad_cols(w_in[:, wide + heads:], LANES)], axis=1).astype(BF16)
    h_main, hs = _in_proj(x, w_in[:, :wide].astype(BF16), w_gates, tm=512, tn=1024, name="gdn_in_proj")
    beta, gcum, gcum_t = _gdn_gates(hs, _pad_row(a_log, LANES), _pad_row(dt_bias, LANES), rows=512)
    return _gdn_core(h_main, conv_w, beta, gcum, gcum_t, norm_g.reshape(1, dk),
                     batch=batch, seq=seq, heads=heads, dk=dk, rows=512, hb=4)


def _fox_layer(x, w_in, b_f, q_norm_g, k_norm_g, *, batch, seq):
    heads = b_f.shape[0]
    dh = q_norm_g.shape[0]
    wide = 4 * heads * dh
    h_main, hf = _in_proj(x, w_in[:, :wide].astype(BF16), _pad_cols(w_in[:, wide:], LANES).astype(BF16),
                          tm=512, tn=1024, name="fox_in_proj")
    qa, ka, vt, edges = _fox_prep(h_main, hf, _pad_row(b_f, LANES),
                                  jnp.tile(q_norm_g, heads).reshape(1, heads * dh),
                                  jnp.tile(k_norm_g, heads).reshape(1, heads * dh),
                                  batch=batch, seq=seq, heads=heads, dh=dh, rows=FOX_TILE)
    return _fox_attn(qa, ka, vt, h_main, edges, q_norm_g, k_norm_g, batch=batch, seq=seq, heads=heads, dh=dh,
                     tile=FOX_TILE, unroll=8)


def kernel(x, p, ln_g, ln_b, ple_w_gate, ple_w_proj, gdn_w_in, gdn_conv_w, gdn_a_log, gdn_dt_bias,
           gdn_norm_g, gdn_w_out, fox_w_in, fox_b_f, fox_q_norm_g, fox_k_norm_g, fox_w_out):
    batch, seq, d = x.shape
    depth = ln_g.shape[0]
    alpha = (2 * depth) ** 0.25
    xf = x.reshape(batch * seq, d)
    for i in range(depth):
        j = i // 2
        if i % 2 == 0:
            a = _gdn_layer(xf, gdn_w_in[j], gdn_conv_w[j], gdn_a_log[j], gdn_dt_bias[j], gdn_norm_g[j],
                           batch=batch, seq=seq)
            w_out = gdn_w_out[j]
        else:
            a = _fox_layer(xf, fox_w_in[j], fox_b_f[j], fox_q_norm_g[j], fox_k_norm_g[j],
                           batch=batch, seq=seq)
            w_out = fox_w_out[j]
        xf = _post(a, xf, p[i].reshape(batch * seq, -1), w_out.astype(BF16), ln_g[i].reshape(1, d),
                   ln_b[i].reshape(1, d), ple_w_gate[i].astype(BF16), ple_w_proj[i].astype(BF16),
                   alpha=alpha, tm=512)
    return xf.reshape(batch, seq, d)
```

```python
import functools

import jax
import jax.numpy as jnp
import numpy as np
from jax import lax
from jax.experimental import pallas as pl
from jax.experimental.pallas import tpu as pltpu

F32 = jnp.float32
BF16 = jnp.bfloat16

LANES = 128
V7X_VMEM_BYTES = 64 * 1024 * 1024
VMEM_LIMIT = V7X_VMEM_BYTES * 3 // 4

GDN_CHUNK = 64
LN_EPS = 1e-5
RMS_EPS = 1e-6
LOG2E = 1.4426950408889634

_NT = (((1,), (1,)), ((), ()))
_TN = (((0,), (0,)), ((), ()))


def _mm(a, b):
    return jnp.dot(a.astype(BF16), b.astype(BF16), preferred_element_type=F32)


def _sigmoid(x):
    return 1.0 / (1.0 + jnp.exp(-x))


def _silu(x):
    return x * _sigmoid(x)


def _softplus(x):
    return jnp.maximum(x, 0.0) + jnp.log1p(jnp.exp(-jnp.abs(x)))


def _split3(x):
    hi = x.astype(BF16).astype(F32)
    r = x - hi
    mid = r.astype(BF16).astype(F32)
    lo = (r - mid).astype(BF16).astype(F32)
    return hi, mid, lo


def _iota2(shape, dim):
    return lax.broadcasted_iota(jnp.int32, shape, dim)


def _params(*semantics):
    return pltpu.CompilerParams(dimension_semantics=semantics, vmem_limit_bytes=VMEM_LIMIT)


def _in_proj_kernel(x_ref, wm_ref, ws_ref, hm_ref, hs_ref, *, tn):
    xb = x_ref[...].astype(BF16)
    for j in range(wm_ref.shape[1] // tn):
        cols = slice(j * tn, (j + 1) * tn)
        hm_ref[:, cols] = jnp.dot(xb, wm_ref[:, cols], preferred_element_type=F32).astype(hm_ref.dtype)
    hs_ref[...] = jnp.dot(xb, ws_ref[...], preferred_element_type=F32)


def _in_proj(x, w_stack, layer, n, w_small, *, tm, tn, name):
    m, k = x.shape
    ns = w_small.shape[1]
    return pl.pallas_call(
        functools.partial(_in_proj_kernel, tn=tn),
        grid=(m // tm,),
        in_specs=[pl.BlockSpec((tm, k), lambda i: (i, 0)),
                  pl.BlockSpec((None, k, n), lambda i: (layer, 0, 0)),
                  pl.BlockSpec((k, ns), lambda i: (0, 0))],
        out_specs=[pl.BlockSpec((tm, n), lambda i: (i, 0)),
                   pl.BlockSpec((tm, ns), lambda i: (i, 0))],
        out_shape=[jax.ShapeDtypeStruct((m, n), BF16), jax.ShapeDtypeStruct((m, ns), F32)],
        compiler_params=_params("parallel"),
        name=name,
    )(x, w_stack, w_small)


def _gdn_gate_kernel(hs_ref, alog_ref, dtb_ref, beta_ref, gcum_ref, gcum_t_ref, *, rows):
    c = GDN_CHUNK
    beta_ref[...] = _sigmoid(hs_ref[:, :LANES])
    g = -jnp.exp(alog_ref[...]) * _softplus(hs_ref[:, LANES:] + dtb_ref[...])
    g3 = jnp.concatenate([part.astype(BF16) for part in _split3(g)], axis=1)
    tril = (_iota2((c, c), 0) >= _iota2((c, c), 1)).astype(BF16)
    eye = (_iota2((8, LANES), 0) == _iota2((8, LANES), 1)).astype(BF16)
    for i in range(rows // c):
        sums = jnp.dot(tril, g3[i * c:(i + 1) * c], preferred_element_type=F32)
        gc = sums[:, :LANES] + sums[:, LANES:2 * LANES] + sums[:, 2 * LANES:]
        gcum_ref[i * c:(i + 1) * c, :] = gc
        hi, mid, lo = (lax.dot_general(eye, part.astype(BF16), _NT, preferred_element_type=F32)
                       for part in _split3(gc))
        gcum_t_ref[i] = hi + mid + lo


def _gdn_gates(hs, alog_row, dtb_row, *, rows):
    m = hs.shape[0]
    c = GDN_CHUNK
    return pl.pallas_call(
        functools.partial(_gdn_gate_kernel, rows=rows),
        grid=(m // rows,),
        in_specs=[pl.BlockSpec((rows, 2 * LANES), lambda i: (i, 0)),
                  pl.BlockSpec((1, LANES), lambda i: (0, 0)),
                  pl.BlockSpec((1, LANES), lambda i: (0, 0))],
        out_specs=[pl.BlockSpec((rows, LANES), lambda i: (i, 0)),
                   pl.BlockSpec((rows, LANES), lambda i: (i, 0)),
                   pl.BlockSpec((rows // c, 8, c), lambda i: (i, 0, 0))],
        out_shape=[jax.ShapeDtypeStruct((m, LANES), F32),
                   jax.ShapeDtypeStruct((m, LANES), F32),
                   jax.ShapeDtypeStruct((m // c, 8, c), F32)],
        compiler_params=_params("parallel"),
        name="gdn_gates",
    )(hs, alog_row, dtb_row)


def _gdn_core_kernel(hq_ref, hk_ref, hv_ref, hz_ref, wq_ref, wk_ref, wv_ref, beta_ref, gcum_ref,
                     gcum_t_ref, ng_ref, o_ref, state_ref, xe_ref, *, rows, dk, hb):
    c = GDN_CHUNK
    n = rows // c
    group = pl.program_id(1)

    @pl.when(pl.program_id(2) == 0)
    def _():
        state_ref[...] = jnp.zeros_like(state_ref)
        xe_ref[:, :8, :] = jnp.zeros((3, 8, xe_ref.shape[2]), F32)

    def conv_silu(x_ref, w_ref, slot):
        w = w_ref[...]
        taps = w.shape[0]
        xe_ref[slot, 8:, :] = x_ref[...].astype(F32)
        y = w[taps - 1:taps, :] * xe_ref[slot, 8:, :]
        for t in range(taps - 1):
            y = y + w[t:t + 1, :] * xe_ref[slot, pl.ds(8 - taps + 1 + t, rows), :]
        xe_ref[slot, :8, :] = xe_ref[slot, rows:, :]
        return _silu(y)

    def l2norm(x):
        return x * lax.rsqrt(jnp.sum(x * x, axis=-1, keepdims=True) + RMS_EPS)

    k_all = conv_silu(hk_ref, wk_ref, 1)
    q_all = conv_silu(hq_ref, wq_ref, 0)
    v_all = conv_silu(hv_ref, wv_ref, 2)
    lane = _iota2((rows, LANES), 1)
    ri = _iota2((c, c), 0)
    ci = _iota2((c, c), 1)
    causal = ri >= ci
    strict = ri > ci
    eye = (ri == ci).astype(F32)
    dv = dk

    units = []
    for a in range(hb):
        head = group * hb + a
        cols = slice(a * dk, (a + 1) * dk)
        k = l2norm(k_all[:, cols])
        q = l2norm(q_all[:, cols]) * (dk ** -0.5)
        v = v_all[:, cols]
        beta = jnp.sum(jnp.where(lane == head, beta_ref[...], 0.0), axis=1, keepdims=True)
        gcol = jnp.sum(jnp.where(lane == head, gcum_ref[...], 0.0), axis=1, keepdims=True)
        for i in range(n):
            sl = slice(i * c, (i + 1) * c)
            units.append(dict(a=a, q=q[sl], k=k[sl], v=v[sl], beta=beta[sl], g=gcol[sl],
                              grow=gcum_t_ref[i, pl.ds(head, 1), :]))
    for u in units:
        u["glast"] = u["g"][c - 1:c, :]
        u["eg"] = jnp.exp(u["g"])
        u["decay"] = jnp.where(causal, jnp.exp(jnp.where(causal, u["g"] - u["grow"], 0.0)), 0.0)
        u["kb"] = u["k"] * u["beta"]
    for u in units:
        kbf = u["k"].astype(BF16)
        u["kk"] = lax.dot_general(u["kb"].astype(BF16), kbf, _NT, preferred_element_type=F32)
        u["qk"] = lax.dot_general(u["q"].astype(BF16), kbf, _NT, preferred_element_type=F32)
    for u in units:
        u["lmat"] = jnp.where(strict, u["kk"] * u["decay"], 0.0)
        u["amat"] = jnp.where(causal, u["qk"] * u["decay"], 0.0)
        u["tinv"] = eye - u["lmat"]
    for u in units:
        u["pk"] = _mm(u["lmat"], u["lmat"])
    span = 4
    while span < c:
        for u in units:
            u["both"] = _mm(u["pk"], jnp.concatenate([u["tinv"], u["pk"]], axis=1))
        for u in units:
            u["tinv"] = u["tinv"] + u["both"][:, :c]
            u["pk"] = u["both"][:, c:]
        span *= 2
    for u in units:
        u["tinv"] = u["tinv"] + _mm(u["pk"], u["tinv"])
    for u in units:
        u["x"] = _mm(u["tinv"], jnp.concatenate([u["v"] * u["beta"], u["kb"] * u["eg"]], axis=1))
    for u in units:
        u["ax"] = _mm(u["amat"], u["x"])
        kd = u["k"] * jnp.exp(u["glast"] - u["g"])
        u["kx"] = lax.dot_general(kd.astype(BF16), u["x"].astype(BF16), _TN, preferred_element_type=F32)
    for u in units:
        u["qeff"] = u["q"] * u["eg"] - u["ax"][:, dv:]
    states = [state_ref[a] for a in range(hb)]
    outs = [[] for _ in range(hb)]
    for i in range(n):
        for a in range(hb):
            u = units[a * n + i]
            sb = states[a].astype(BF16)
            outs[a].append(_mm(u["qeff"], sb) + u["ax"][:, :dv])
            states[a] = jnp.exp(u["glast"]) * states[a] + u["kx"][:, :dv] - _mm(u["kx"][:, dv:], sb)
    for a in range(hb):
        state_ref[a] = states[a]
        o = jnp.concatenate(outs[a], axis=0)
        o = o * lax.rsqrt(jnp.mean(o * o, axis=-1, keepdims=True) + RMS_EPS) * ng_ref[...]
        cols = slice(a * dk, (a + 1) * dk)
        o_ref[:, cols] = (o * _silu(hz_ref[:, cols].astype(F32))).astype(o_ref.dtype)


def _gdn_core(h_main, conv_w, beta, gcum, gcum_t, ng_row, *, batch, seq, heads, dk, rows, hb):
    m = h_main.shape[0]
    nb = seq // rows
    c = GDN_CHUNK
    taps = conv_w.shape[0]
    groups = heads // hb
    wide = hb * dk

    def col(off):
        return pl.BlockSpec((rows, wide), lambda b, h, j: (b * nb + j, off + h))

    def wcol(off):
        return pl.BlockSpec((taps, wide), lambda b, h, j: (0, off + h))

    def gate():
        return pl.BlockSpec((rows, LANES), lambda b, h, j: (b * nb + j, 0))

    return pl.pallas_call(
        functools.partial(_gdn_core_kernel, rows=rows, dk=dk, hb=hb),
        grid=(batch, groups, nb),
        in_specs=[col(0), col(groups), col(2 * groups), col(3 * groups),
                  wcol(0), wcol(groups), wcol(2 * groups),
                  gate(), gate(),
                  pl.BlockSpec((rows // c, 8, c), lambda b, h, j: (b * nb + j, 0, 0)),
                  pl.BlockSpec((1, dk), lambda b, h, j: (0, 0))],
        out_specs=pl.BlockSpec((rows, wide), lambda b, h, j: (b * nb + j, h)),
        out_shape=jax.ShapeDtypeStruct((m, heads * dk), BF16),
        scratch_shapes=[pltpu.VMEM((hb, dk, dk), F32), pltpu.VMEM((3, rows + 8, wide), F32)],
        compiler_params=_params("parallel", "parallel", "arbitrary"),
        name="gdn_core",
    )(h_main, h_main, h_main, h_main, conv_w, conv_w, conv_w, beta, gcum, gcum_t, ng_row)


FOX_TILE = 512
FOX_VROWS = 80


def _fox_prep_kernel(hq_ref, hk_ref, hv_ref, hf_ref, bf_ref, qg_ref, kg_ref, selq_ref, selk_ref,
                     qa_ref, ka_ref, vt_ref, edge_ref, carry_ref, *, rows, heads, dh):
    @pl.when(pl.program_id(1) == 0)
    def _():
        carry_ref[...] = jnp.zeros_like(carry_ref)

    logf = -_softplus(-(hf_ref[...] + bf_ref[...]))
    lane = _iota2((rows, LANES), 1)
    logf = jnp.where(lane < heads, logf, 0.0)
    tril = (_iota2((rows, rows), 0) >= _iota2((rows, rows), 1)).astype(BF16)
    sums = jnp.dot(tril, jnp.concatenate([part.astype(BF16) for part in _split3(logf)], axis=1),
                   preferred_element_type=F32)
    c = sums[:, :LANES] + sums[:, LANES:2 * LANES] + sums[:, 2 * LANES:] + carry_ref[...]
    carry_ref[...] = c[rows - 1:rows, :]
    c2 = c * LOG2E
    edge_ref[0] = jnp.concatenate([c2[:1], c2[rows - 1:], jnp.zeros((6, LANES), F32)], axis=0)
    chi, cmid, clo = _split3(c2)
    parts = (chi + pltpu.roll(cmid, heads, axis=1) + pltpu.roll(clo, 2 * heads, axis=1)
             + jnp.where(lane == 3 * heads, 1.0, 0.0)).astype(BF16)
    q_tails = jnp.dot(parts, selq_ref[...], preferred_element_type=F32)
    k_tails = jnp.dot(parts, selk_ref[...], preferred_element_type=F32)

    same_head = (_iota2((LANES, LANES), 0) // dh == _iota2((LANES, LANES), 1) // dh).astype(BF16)

    def head_rms(x, g, scale):
        ssum = jnp.dot((x * x).astype(BF16), same_head, preferred_element_type=F32)
        return x * lax.rsqrt(ssum * (1.0 / dh) + RMS_EPS) * (g * scale)

    per_block = LANES // dh
    vrow = _iota2((FOX_VROWS, LANES), 0)
    vlane = _iota2((FOX_VROWS, LANES), 1)
    out_row = _iota2((FOX_VROWS, rows), 0)
    for j in range(heads // per_block):
        cols = slice(j * LANES, (j + 1) * LANES)
        qn = head_rms(hq_ref[:, cols].astype(F32), qg_ref[:, cols], dh ** -0.5 * LOG2E)
        kn = head_rms(hk_ref[:, cols].astype(F32), kg_ref[:, cols], 1.0)
        vb = hv_ref[:, cols].astype(BF16)
        for a in range(per_block):
            h = j * per_block + a
            qh = qn if a == 0 else pltpu.roll(qn, LANES - a * dh, axis=1)
            kh = kn if a == 0 else pltpu.roll(kn, LANES - a * dh, axis=1)
            out_cols = slice(h * LANES, (h + 1) * LANES)
            qa_ref[:, out_cols] = jnp.where(lane < dh, qh, q_tails[:, out_cols]).astype(qa_ref.dtype)
            ka_ref[:, out_cols] = jnp.where(lane < dh, kh, k_tails[:, out_cols]).astype(ka_ref.dtype)
            pick = ((vlane == a * dh + vrow) & (vrow < dh)).astype(BF16)
            vt = lax.dot_general(pick, vb, _NT, preferred_element_type=F32)
            vt_ref[0, h] = jnp.where(out_row == dh, 1.0, vt).astype(vt_ref.dtype)


def _fox_bias_selectors(heads, dh):
    j = np.arange(LANES)[:, None]
    col = np.arange(heads * LANES)[None, :]
    h, l = col // LANES, col % LANES - dh
    part = lambda i: (l == i) & (j == i * heads + h)
    ones = lambda lo: (l >= lo) & (l < lo + 3) & (j == 3 * heads)
    sel_q = part(0) | part(1) | part(2) | ones(3)
    sel_k = ones(0).astype(np.float32) - ((l == 3) & (j == h)) - ((l == 4) & (j == heads + h)) \
        - ((l == 5) & (j == 2 * heads + h))
    return jnp.asarray(sel_q, BF16), jnp.asarray(sel_k, BF16)


def _fox_prep(h_main, hf, bf_row, qg_row, kg_row, *, batch, seq, heads, dh, rows):
    m = h_main.shape[0]
    w = heads * dh
    nt = seq // rows
    sel_q, sel_k = _fox_bias_selectors(heads, dh)

    def col(j):
        return pl.BlockSpec((rows, w), lambda b, t: (b * nt + t, j))

    def const(r, n):
        return pl.BlockSpec((r, n), lambda b, t: (0, 0))

    aug = pl.BlockSpec((rows, heads * LANES), lambda b, t: (b * nt + t, 0))
    return pl.pallas_call(
        functools.partial(_fox_prep_kernel, rows=rows, heads=heads, dh=dh),
        grid=(batch, nt),
        in_specs=[col(0), col(1), col(2),
                  pl.BlockSpec((rows, LANES), lambda b, t: (b * nt + t, 0)),
                  const(1, LANES), const(1, w), const(1, w),
                  const(LANES, heads * LANES), const(LANES, heads * LANES)],
        out_specs=[aug, aug,
                   pl.BlockSpec((1, heads, FOX_VROWS, rows), lambda b, t: (b, 0, 0, t)),
                   pl.BlockSpec((1, 8, LANES), lambda b, t: (b * nt + t, 0, 0))],
        out_shape=[jax.ShapeDtypeStruct((m, heads * LANES), BF16),
                   jax.ShapeDtypeStruct((m, heads * LANES), BF16),
                   jax.ShapeDtypeStruct((batch, heads, FOX_VROWS, seq), BF16),
                   jax.ShapeDtypeStruct((batch * nt, 8, LANES), F32)],
        scratch_shapes=[pltpu.VMEM((1, LANES), F32)],
        compiler_params=_params("parallel", "arbitrary"),
        name="fox_prep",
    )(h_main, h_main, h_main, hf, bf_row, qg_row, kg_row, sel_q, sel_k)


def _fox_attn_kernel(tq_ref, tk_ref, steps_ref, q_ref, k_ref, v_ref, z_ref, o_ref, s0_ref, s1_ref, m_ref, acc_ref,
                     *, tile, dh, per_block, nq, unroll, heads, tasks):
    below_diag = _iota2((tile, tile), 0) <= _iota2((tile, tile), 1)

    def logits(a, qi, kj, s_ref):
        qs = pl.multiple_of(jnp.minimum(qi, nq - 1) * tile, tile)
        ks = pl.multiple_of(kj * tile, tile)
        s_ref[a] = lax.dot_general(k_ref[pl.ds(ks, tile), a * LANES:(a + 1) * LANES],
                                   q_ref[pl.ds(qs, tile), a * LANES:(a + 1) * LANES], _NT,
                                   preferred_element_type=F32)

    def softmax_pv(a, qi, kj, s_ref, diagonal):
        ks = pl.multiple_of(kj * tile, tile)
        vt = v_ref[0, a, :, pl.ds(ks, tile)]
        if diagonal:
            s = jnp.where(below_diag, s_ref[a], -jnp.inf)
            m_new = jnp.max(s, axis=0, keepdims=True)
            p = jnp.exp2(s - m_new).astype(BF16)
            acc_ref[qi, a] = jnp.dot(vt, p, preferred_element_type=F32)
        else:
            s = s_ref[a]
            m_prev = m_ref[qi, a]
            m_new = jnp.maximum(m_prev, jnp.max(s, axis=0, keepdims=True))
            p = jnp.exp2(s - m_new).astype(BF16)
            acc_ref[qi, a] = (jnp.exp2(m_prev - m_new) * acc_ref[qi, a]
                              + jnp.dot(vt, p, preferred_element_type=F32))
        m_ref[qi, a] = m_new

    bufs = (s0_ref, s1_ref)
    everyone = range(per_block)
    m_ref[nq] = jnp.zeros(m_ref.shape[1:], F32)
    acc_ref[nq] = jnp.zeros(acc_ref.shape[1:], F32)

    for a in everyone:
        logits(a, 0, 0, s0_ref)

    def diagonal_steps(u, _):
        for r in range(unroll):
            d = unroll * u + r
            nxt = jnp.minimum(d + 1, nq - 1)
            for a in everyone:
                logits(a, nxt, nxt, bufs[(r + 1) % 2])
            for a in everyone:
                softmax_pv(a, d, d, bufs[r % 2], True)
        return 0

    lax.fori_loop(0, nq // unroll, diagonal_steps, 0)

    for a in everyone:
        head = pl.program_id(1) * per_block + a
        base = (pl.program_id(0) * heads + head) * tasks
        logits(a, tq_ref[base], tk_ref[base], s0_ref)

        def lower_steps(u, _, a=a, base=base):
            for r in range(unroll):
                t = base + unroll * u + r
                nxt = jnp.minimum(t + 1, base + tasks - 1)
                logits(a, tq_ref[nxt], tk_ref[nxt], bufs[(r + 1) % 2])
                softmax_pv(a, tq_ref[t], tk_ref[t], bufs[r % 2], False)
            return 0

        lax.fori_loop(0, steps_ref[pl.program_id(0) * heads + head], lower_steps, 0)

    def finish(qi, _):
        qs = pl.multiple_of(qi * tile, tile)
        o_t = jnp.concatenate([acc_ref[qi, a, :dh, :] / acc_ref[qi, a, dh:dh + 1, :] for a in everyone],
                              axis=0)
        o_ref[pl.ds(qs, tile), :] = (o_t.T * _silu(z_ref[pl.ds(qs, tile), :].astype(F32))).astype(o_ref.dtype)
        return 0

    lax.fori_loop(0, nq, finish, 0)


FOX_UNDERFLOW_LOG2 = 150.0
FOX_NORM_SLACK = 1.02
FOX_BIAS_SLACK = 2.0


def _fox_task_lists(edges, q_gain, k_gain, *, batch, heads, dh, nq, unroll):
    qk_bound = FOX_NORM_SLACK * dh ** 0.5 * LOG2E * jnp.max(jnp.abs(q_gain)) * jnp.max(jnp.abs(k_gain))
    threshold = 2.0 * qk_bound + FOX_UNDERFLOW_LOG2 + FOX_BIAS_SLACK
    edges = edges.reshape(batch, nq, 8, LANES)
    first = edges[:, :, 0, :heads].transpose(0, 2, 1)
    last = edges[:, :, 1, :heads].transpose(0, 2, 1)
    qi = jnp.arange(nq)[:, None]
    kj = jnp.arange(nq)[None, :]
    keep = (kj < qi) & (first[..., :, None] - last[..., None, :] >= -threshold)
    start = jnp.min(jnp.where(keep, kj, qi), axis=-1)
    length = jnp.arange(nq) - start
    ends = jnp.cumsum(length, axis=-1)
    count = ends[..., -1]
    tasks = nq * (nq - 1) // 2
    t = jnp.arange(tasks)
    row = jnp.sum(ends[..., None, :] <= t[:, None], axis=-1)
    in_row = row[..., None] == jnp.arange(nq)
    tk = t + jnp.sum(jnp.where(in_row, (start - ends + length)[..., None, :], 0), axis=-1)
    tq = jnp.where(t < count[..., None], row, nq).astype(jnp.int32)
    tk = jnp.where(t < count[..., None], tk, 0).astype(jnp.int32)
    steps = ((count + unroll - 1) // unroll).astype(jnp.int32)
    return tq.reshape(-1), tk.reshape(-1), steps.reshape(-1), tasks


def _fox_attn(qa, ka, vt, h_main, edges, q_gain, k_gain, *, batch, seq, heads, dh, tile, unroll):
    m = qa.shape[0]
    nq = seq // tile
    assert unroll % 2 == 0 and nq % unroll == 0 and (nq * (nq - 1) // 2) % unroll == 0
    per_block = LANES // dh
    blocks = heads // per_block
    z_off = 3 * blocks
    wide = per_block * LANES
    tq, tk, steps, tasks = _fox_task_lists(edges, q_gain, k_gain, batch=batch, heads=heads, dh=dh, nq=nq,
                                           unroll=unroll)
    grid_spec = pltpu.PrefetchScalarGridSpec(
        num_scalar_prefetch=3,
        grid=(batch, blocks),
        in_specs=[pl.BlockSpec((seq, wide), lambda b, h, *_: (b, h)),
                  pl.BlockSpec((seq, wide), lambda b, h, *_: (b, h)),
                  pl.BlockSpec((1, per_block, FOX_VROWS, seq), lambda b, h, *_: (b, h, 0, 0)),
                  pl.BlockSpec((seq, LANES), lambda b, h, *_: (b, z_off + h))],
        out_specs=pl.BlockSpec((seq, LANES), lambda b, h, *_: (b, h)),
        scratch_shapes=[pltpu.VMEM((per_block, tile, tile), F32), pltpu.VMEM((per_block, tile, tile), F32),
                        pltpu.VMEM((nq + 1, per_block, 1, tile), F32),
                        pltpu.VMEM((nq + 1, per_block, FOX_VROWS, tile), F32)])
    return pl.pallas_call(
        functools.partial(_fox_attn_kernel, tile=tile, dh=dh, per_block=per_block, nq=nq, unroll=unroll,
                          heads=heads, tasks=tasks),
        grid_spec=grid_spec,
        out_shape=jax.ShapeDtypeStruct((m, heads * dh), BF16),
        compiler_params=_params("parallel", "parallel"),
        name="fox_attn",
    )(tq, tk, steps, qa, ka, vt, h_main)


def _post_kernel(a_ref, x_ref, p_ref, wo_ref, g_ref, b_ref, wg_ref, wp_ref, o_ref, *, alpha):
    t = alpha * x_ref[...] + jnp.dot(a_ref[...], wo_ref[...], preferred_element_type=F32)
    mu = jnp.mean(t, axis=-1, keepdims=True)
    d = t - mu
    var = jnp.mean(d * d, axis=-1, keepdims=True)
    xn = d * lax.rsqrt(var + LN_EPS) * g_ref[...] + b_ref[...]
    gate = _sigmoid(_mm(xn, wg_ref[...]))
    o_ref[...] = xn + gate * _mm(p_ref[...], wp_ref[...])


def _post(a, x, p, w_out, ln_g, ln_b, w_gate, w_proj, *, layer, mixer, alpha, tm):
    m, d = x.shape
    dp = p.shape[2]

    def rows(n):
        return pl.BlockSpec((tm, n), lambda i: (i, 0))

    def of_layer(index, r, n):
        return pl.BlockSpec((None, r, n), lambda i: (index, 0, 0))

    return pl.pallas_call(
        functools.partial(_post_kernel, alpha=alpha),
        grid=(m // tm,),
        in_specs=[rows(a.shape[1]), rows(d), pl.BlockSpec((None, tm, dp), lambda i: (layer, i, 0)),
                  of_layer(mixer, a.shape[1], d), of_layer(layer, 1, d), of_layer(layer, 1, d),
                  of_layer(layer, d, d), of_layer(layer, dp, d)],
        out_specs=rows(d),
        out_shape=jax.ShapeDtypeStruct((m, d), F32),
        compiler_params=_params("parallel"),
        name="post_block",
    )(a, x, p, w_out, ln_g, ln_b, w_gate, w_proj)


def _pad_cols(w, n):
    return jnp.pad(w, ((0, 0), (0, n - w.shape[1])))


def _pad_row(v, n):
    return jnp.pad(v, (0, n - v.shape[0])).reshape(1, n)


def _gdn_layer(x, w_stack, layer, w_in, conv_w, a_log, dt_bias, norm_g, *, batch, seq):
    heads = a_log.shape[0]
    dk = norm_g.shape[0]
    wide = 4 * heads * dk
    w_gates = jnp.concatenate([_pad_cols(w_in[:, wide:wide + heads], LANES),
                               _pad_cols(w_in[:, wide + heads:], LANES)], axis=1).astype(BF16)
    h_main, hs = _in_proj(x, w_stack, layer, wide, w_gates, tm=512, tn=1024, name="gdn_in_proj")
    beta, gcum, gcum_t = _gdn_gates(hs, _pad_row(a_log, LANES), _pad_row(dt_bias, LANES), rows=512)
    return _gdn_core(h_main, conv_w, beta, gcum, gcum_t, norm_g.reshape(1, dk),
                     batch=batch, seq=seq, heads=heads, dk=dk, rows=512, hb=4)


def _fox_layer(x, w_stack, layer, w_in, b_f, q_norm_g, k_norm_g, *, batch, seq):
    heads = b_f.shape[0]
    dh = q_norm_g.shape[0]
    wide = 4 * heads * dh
    h_main, hf = _in_proj(x, w_stack, layer, wide, _pad_cols(w_in[:, wide:], LANES).astype(BF16),
                          tm=512, tn=1024, name="fox_in_proj")
    qa, ka, vt, edges = _fox_prep(h_main, hf, _pad_row(b_f, LANES),
                                  jnp.tile(q_norm_g, heads).reshape(1, heads * dh),
                                  jnp.tile(k_norm_g, heads).reshape(1, heads * dh),
                                  batch=batch, seq=seq, heads=heads, dh=dh, rows=FOX_TILE)
    return _fox_attn(qa, ka, vt, h_main, edges, q_norm_g, k_norm_g, batch=batch, seq=seq, heads=heads, dh=dh,
                     tile=FOX_TILE, unroll=8)


def kernel(x, p, ln_g, ln_b, ple_w_gate, ple_w_proj, gdn_w_in, gdn_conv_w, gdn_a_log, gdn_dt_bias,
           gdn_norm_g, gdn_w_out, fox_w_in, fox_b_f, fox_q_norm_g, fox_k_norm_g, fox_w_out):
    batch, seq, d = x.shape
    depth = ln_g.shape[0]
    alpha = (2 * depth) ** 0.25
    xf = x.reshape(batch * seq, d)
    p_rows = p.reshape(depth, batch * seq, -1)
    ln_g3, ln_b3 = ln_g.reshape(depth, 1, d), ln_b.reshape(depth, 1, d)
    w_gate, w_proj = ple_w_gate.astype(BF16), ple_w_proj.astype(BF16)
    gdn_w, fox_w = gdn_w_in.astype(BF16), fox_w_in.astype(BF16)
    gdn_wo, fox_wo = gdn_w_out.astype(BF16), fox_w_out.astype(BF16)
    for i in range(depth):
        j = i // 2
        if i % 2 == 0:
            a = _gdn_layer(xf, gdn_w, j, gdn_w_in[j], gdn_conv_w[j], gdn_a_log[j], gdn_dt_bias[j], gdn_norm_g[j],
                           batch=batch, seq=seq)
            w_out = gdn_wo
        else:
            a = _fox_layer(xf, fox_w, j, fox_w_in[j], fox_b_f[j], fox_q_norm_g[j], fox_k_norm_g[j],
                           batch=batch, seq=seq)
            w_out = fox_wo
        xf = _post(a, xf, p_rows, w_out, ln_g3, ln_b3, w_gate, w_proj, layer=i, mixer=j, alpha=alpha, tm=512)
    return xf.reshape(batch, seq, d)
```

```python
import functools

import jax
import jax.numpy as jnp
import numpy as np
from jax import lax
from jax.experimental import pallas as pl
from jax.experimental.pallas import tpu as pltpu

F32 = jnp.float32
BF16 = jnp.bfloat16

LANES = 128
V7X_VMEM_BYTES = 64 * 1024 * 1024
VMEM_LIMIT = V7X_VMEM_BYTES * 3 // 4

GDN_CHUNK = 64
LN_EPS = 1e-5
RMS_EPS = 1e-6
LOG2E = 1.4426950408889634

_NT = (((1,), (1,)), ((), ()))
_TN = (((0,), (0,)), ((), ()))


def _mm(a, b):
    return jnp.dot(a.astype(BF16), b.astype(BF16), preferred_element_type=F32)


def _sigmoid(x):
    return 1.0 / (1.0 + jnp.exp(-x))


def _silu(x):
    h = 0.5 * x
    return h + h * jnp.tanh(h)


def _softplus(x):
    return jnp.maximum(x, 0.0) + jnp.log1p(jnp.exp(-jnp.abs(x)))


def _split3(x):
    hi = x.astype(BF16).astype(F32)
    r = x - hi
    mid = r.astype(BF16).astype(F32)
    lo = (r - mid).astype(BF16).astype(F32)
    return hi, mid, lo


def _iota2(shape, dim):
    return lax.broadcasted_iota(jnp.int32, shape, dim)


def _params(*semantics):
    return pltpu.CompilerParams(dimension_semantics=semantics, vmem_limit_bytes=VMEM_LIMIT)


def _in_proj_kernel(x_ref, wm_ref, ws_ref, hm_ref, hs_ref, *, tn):
    xb = x_ref[...].astype(BF16)
    for j in range(wm_ref.shape[1] // tn):
        cols = slice(j * tn, (j + 1) * tn)
        hm_ref[:, cols] = jnp.dot(xb, wm_ref[:, cols], preferred_element_type=F32).astype(hm_ref.dtype)
    hs_ref[...] = jnp.dot(xb, ws_ref[...], preferred_element_type=F32)


def _in_proj(x, w_stack, layer, n, w_small, *, tm, tn, name):
    m, k = x.shape
    ns = w_small.shape[1]
    return pl.pallas_call(
        functools.partial(_in_proj_kernel, tn=tn),
        grid=(m // tm,),
        in_specs=[pl.BlockSpec((tm, k), lambda i: (i, 0)),
                  pl.BlockSpec((None, k, n), lambda i: (layer, 0, 0)),
                  pl.BlockSpec((k, ns), lambda i: (0, 0))],
        out_specs=[pl.BlockSpec((tm, n), lambda i: (i, 0)),
                   pl.BlockSpec((tm, ns), lambda i: (i, 0))],
        out_shape=[jax.ShapeDtypeStruct((m, n), BF16), jax.ShapeDtypeStruct((m, ns), F32)],
        compiler_params=_params("parallel"),
        name=name,
    )(x, w_stack, w_small)


def _gdn_gate_kernel(hs_ref, alog_ref, dtb_ref, beta_ref, gcum_ref, gcum_t_ref, *, rows):
    c = GDN_CHUNK
    beta_ref[...] = _sigmoid(hs_ref[:, :LANES])
    g = -jnp.exp(alog_ref[...]) * _softplus(hs_ref[:, LANES:] + dtb_ref[...])
    g3 = jnp.concatenate([part.astype(BF16) for part in _split3(g)], axis=1)
    tril = (_iota2((c, c), 0) >= _iota2((c, c), 1)).astype(BF16)
    eye = (_iota2((8, LANES), 0) == _iota2((8, LANES), 1)).astype(BF16)
    for i in range(rows // c):
        sums = jnp.dot(tril, g3[i * c:(i + 1) * c], preferred_element_type=F32)
        gc = sums[:, :LANES] + sums[:, LANES:2 * LANES] + sums[:, 2 * LANES:]
        gcum_ref[i * c:(i + 1) * c, :] = gc
        hi, mid, lo = (lax.dot_general(eye, part.astype(BF16), _NT, preferred_element_type=F32)
                       for part in _split3(gc))
        gcum_t_ref[i] = hi + mid + lo


def _gdn_gates(hs, alog_row, dtb_row, *, rows):
    m = hs.shape[0]
    c = GDN_CHUNK
    return pl.pallas_call(
        functools.partial(_gdn_gate_kernel, rows=rows),
        grid=(m // rows,),
        in_specs=[pl.BlockSpec((rows, 2 * LANES), lambda i: (i, 0)),
                  pl.BlockSpec((1, LANES), lambda i: (0, 0)),
                  pl.BlockSpec((1, LANES), lambda i: (0, 0))],
        out_specs=[pl.BlockSpec((rows, LANES), lambda i: (i, 0)),
                   pl.BlockSpec((rows, LANES), lambda i: (i, 0)),
                   pl.BlockSpec((rows // c, 8, c), lambda i: (i, 0, 0))],
        out_shape=[jax.ShapeDtypeStruct((m, LANES), F32),
                   jax.ShapeDtypeStruct((m, LANES), F32),
                   jax.ShapeDtypeStruct((m // c, 8, c), F32)],
        compiler_params=_params("parallel"),
        name="gdn_gates",
    )(hs, alog_row, dtb_row)


def _gdn_core_kernel(hq_ref, hk_ref, hv_ref, hz_ref, wq_ref, wk_ref, wv_ref, beta_ref, gcum_ref,
                     gcum_t_ref, ng_ref, o_ref, state_ref, xe_ref, *, rows, dk, hb):
    c = GDN_CHUNK
    n = rows // c
    group = pl.program_id(1)

    @pl.when(pl.program_id(2) == 0)
    def _():
        state_ref[...] = jnp.zeros_like(state_ref)
        xe_ref[:, :8, :] = jnp.zeros((3, 8, xe_ref.shape[2]), F32)

    def conv_silu(x_ref, w_ref, slot):
        w = w_ref[...]
        taps = w.shape[0]
        xe_ref[slot, 8:, :] = x_ref[...].astype(F32)
        y = w[taps - 1:taps, :] * xe_ref[slot, 8:, :]
        for t in range(taps - 1):
            y = y + w[t:t + 1, :] * xe_ref[slot, pl.ds(8 - taps + 1 + t, rows), :]
        xe_ref[slot, :8, :] = xe_ref[slot, rows:, :]
        return _silu(y)

    def l2norm(x):
        return x * lax.rsqrt(jnp.sum(x * x, axis=-1, keepdims=True) + RMS_EPS)

    k_all = conv_silu(hk_ref, wk_ref, 1)
    q_all = conv_silu(hq_ref, wq_ref, 0)
    v_all = conv_silu(hv_ref, wv_ref, 2)
    lane = _iota2((rows, LANES), 1)
    ri = _iota2((c, c), 0)
    ci = _iota2((c, c), 1)
    causal = ri >= ci
    strict = ri > ci
    eye = (ri == ci).astype(F32)
    dv = dk

    units = []
    for a in range(hb):
        head = group * hb + a
        cols = slice(a * dk, (a + 1) * dk)
        k = l2norm(k_all[:, cols])
        q = l2norm(q_all[:, cols]) * (dk ** -0.5)
        v = v_all[:, cols]
        beta = jnp.sum(jnp.where(lane == head, beta_ref[...], 0.0), axis=1, keepdims=True)
        gcol = jnp.sum(jnp.where(lane == head, gcum_ref[...], 0.0), axis=1, keepdims=True)
        for i in range(n):
            sl = slice(i * c, (i + 1) * c)
            units.append(dict(a=a, q=q[sl], k=k[sl], v=v[sl], beta=beta[sl], g=gcol[sl],
                              grow=gcum_t_ref[i, pl.ds(head, 1), :]))
    for u in units:
        u["glast"] = u["g"][c - 1:c, :]
        u["eg"] = jnp.exp(u["g"])
        u["decay"] = jnp.where(causal, jnp.exp(u["g"] - u["grow"]), 0.0)
        u["kb"] = u["k"] * u["beta"]
    for u in units:
        kbf = u["k"].astype(BF16)
        u["kk"] = lax.dot_general(u["kb"].astype(BF16), kbf, _NT, preferred_element_type=F32)
        u["qk"] = lax.dot_general(u["q"].astype(BF16), kbf, _NT, preferred_element_type=F32)
    for u in units:
        u["lmat"] = jnp.where(strict, u["kk"] * u["decay"], 0.0)
        u["amat"] = jnp.where(causal, u["qk"] * u["decay"], 0.0)
        u["tinv"] = eye - u["lmat"]
    for u in units:
        u["pk"] = _mm(u["lmat"], u["lmat"])
    span = 4
    while span < c:
        for u in units:
            u["both"] = _mm(u["pk"], jnp.concatenate([u["tinv"], u["pk"]], axis=1))
        for u in units:
            u["tinv"] = u["tinv"] + u["both"][:, :c]
            u["pk"] = u["both"][:, c:]
        span *= 2
    for u in units:
        u["tinv"] = u["tinv"] + _mm(u["pk"], u["tinv"])
    for u in units:
        u["x"] = _mm(u["tinv"], jnp.concatenate([u["v"] * u["beta"], u["kb"] * u["eg"]], axis=1))
    for u in units:
        u["ax"] = _mm(u["amat"], u["x"])
        kd = u["k"] * jnp.exp(u["glast"] - u["g"])
        u["kx"] = lax.dot_general(kd.astype(BF16), u["x"].astype(BF16), _TN, preferred_element_type=F32)
    for u in units:
        u["qeff"] = u["q"] * u["eg"] - u["ax"][:, dv:]
    states = [state_ref[a] for a in range(hb)]
    outs = [[] for _ in range(hb)]
    for i in range(n):
        for a in range(hb):
            u = units[a * n + i]
            sb = states[a].astype(BF16)
            outs[a].append(_mm(u["qeff"], sb) + u["ax"][:, :dv])
            states[a] = jnp.exp(u["glast"]) * states[a] + u["kx"][:, :dv] - _mm(u["kx"][:, dv:], sb)
    for a in range(hb):
        state_ref[a] = states[a]
        o = jnp.concatenate(outs[a], axis=0)
        o = o * lax.rsqrt(jnp.mean(o * o, axis=-1, keepdims=True) + RMS_EPS) * ng_ref[...]
        cols = slice(a * dk, (a + 1) * dk)
        o_ref[:, cols] = (o * _silu(hz_ref[:, cols].astype(F32))).astype(o_ref.dtype)


def _gdn_core(h_main, conv_w, beta, gcum, gcum_t, ng_row, *, batch, seq, heads, dk, rows, hb):
    m = h_main.shape[0]
    nb = seq // rows
    c = GDN_CHUNK
    taps = conv_w.shape[0]
    groups = heads // hb
    wide = hb * dk

    def col(off):
        return pl.BlockSpec((rows, wide), lambda b, h, j: (b * nb + j, off + h))

    def wcol(off):
        return pl.BlockSpec((taps, wide), lambda b, h, j: (0, off + h))

    def gate():
        return pl.BlockSpec((rows, LANES), lambda b, h, j: (b * nb + j, 0))

    return pl.pallas_call(
        functools.partial(_gdn_core_kernel, rows=rows, dk=dk, hb=hb),
        grid=(batch, groups, nb),
        in_specs=[col(0), col(groups), col(2 * groups), col(3 * groups),
                  wcol(0), wcol(groups), wcol(2 * groups),
                  gate(), gate(),
                  pl.BlockSpec((rows // c, 8, c), lambda b, h, j: (b * nb + j, 0, 0)),
                  pl.BlockSpec((1, dk), lambda b, h, j: (0, 0))],
        out_specs=pl.BlockSpec((rows, wide), lambda b, h, j: (b * nb + j, h)),
        out_shape=jax.ShapeDtypeStruct((m, heads * dk), BF16),
        scratch_shapes=[pltpu.VMEM((hb, dk, dk), F32), pltpu.VMEM((3, rows + 8, wide), F32)],
        compiler_params=_params("parallel", "parallel", "arbitrary"),
        name="gdn_core",
    )(h_main, h_main, h_main, h_main, conv_w, conv_w, conv_w, beta, gcum, gcum_t, ng_row)


FOX_TILE = 512
FOX_VROWS = 80


def _fox_prep_kernel(hq_ref, hk_ref, hv_ref, hf_ref, bf_ref, qg_ref, kg_ref, selq_ref, selk_ref,
                     qa_ref, ka_ref, vt_ref, edge_ref, carry_ref, *, rows, heads, dh):
    @pl.when(pl.program_id(1) == 0)
    def _():
        carry_ref[...] = jnp.zeros_like(carry_ref)

    logf = -_softplus(-(hf_ref[...] + bf_ref[...]))
    lane = _iota2((rows, LANES), 1)
    logf = jnp.where(lane < heads, logf, 0.0)
    tril = (_iota2((rows, rows), 0) >= _iota2((rows, rows), 1)).astype(BF16)
    sums = jnp.dot(tril, jnp.concatenate([part.astype(BF16) for part in _split3(logf)], axis=1),
                   preferred_element_type=F32)
    c = sums[:, :LANES] + sums[:, LANES:2 * LANES] + sums[:, 2 * LANES:] + carry_ref[...]
    carry_ref[...] = c[rows - 1:rows, :]
    c2 = c * LOG2E
    edge_ref[0] = jnp.concatenate([c2[:1], c2[rows - 1:], jnp.zeros((6, LANES), F32)], axis=0)
    chi, cmid, clo = _split3(c2)
    parts = (chi + pltpu.roll(cmid, heads, axis=1) + pltpu.roll(clo, 2 * heads, axis=1)
             + jnp.where(lane == 3 * heads, 1.0, 0.0)).astype(BF16)
    q_tails = jnp.dot(parts, selq_ref[...], preferred_element_type=F32)
    k_tails = jnp.dot(parts, selk_ref[...], preferred_element_type=F32)

    same_head = (_iota2((LANES, LANES), 0) // dh == _iota2((LANES, LANES), 1) // dh).astype(BF16)

    def head_rms(x, g, scale):
        ssum = jnp.dot((x * x).astype(BF16), same_head, preferred_element_type=F32)
        return x * lax.rsqrt(ssum * (1.0 / dh) + RMS_EPS) * (g * scale)

    per_block = LANES // dh
    vrow = _iota2((FOX_VROWS, LANES), 0)
    vlane = _iota2((FOX_VROWS, LANES), 1)
    out_row = _iota2((FOX_VROWS, rows), 0)
    for j in range(heads // per_block):
        cols = slice(j * LANES, (j + 1) * LANES)
        qn = head_rms(hq_ref[:, cols].astype(F32), qg_ref[:, cols], dh ** -0.5 * LOG2E)
        kn = head_rms(hk_ref[:, cols].astype(F32), kg_ref[:, cols], 1.0)
        vb = hv_ref[:, cols].astype(BF16)
        for a in range(per_block):
            h = j * per_block + a
            qh = qn if a == 0 else pltpu.roll(qn, LANES - a * dh, axis=1)
            kh = kn if a == 0 else pltpu.roll(kn, LANES - a * dh, axis=1)
            out_cols = slice(h * LANES, (h + 1) * LANES)
            qa_ref[:, out_cols] = jnp.where(lane < dh, qh, q_tails[:, out_cols]).astype(qa_ref.dtype)
            ka_ref[:, out_cols] = jnp.where(lane < dh, kh, k_tails[:, out_cols]).astype(ka_ref.dtype)
            pick = ((vlane == a * dh + vrow) & (vrow < dh)).astype(BF16)
            vt = lax.dot_general(pick, vb, _NT, preferred_element_type=F32)
            vt_ref[0, h] = jnp.where(out_row == dh, 1.0, vt).astype(vt_ref.dtype)


def _fox_bias_selectors(heads, dh):
    j = np.arange(LANES)[:, None]
    col = np.arange(heads * LANES)[None, :]
    h, l = col // LANES, col % LANES - dh
    part = lambda i: (l == i) & (j == i * heads + h)
    ones = lambda lo: (l >= lo) & (l < lo + 3) & (j == 3 * heads)
    sel_q = part(0) | part(1) | part(2) | ones(3)
    sel_k = ones(0).astype(np.float32) - ((l == 3) & (j == h)) - ((l == 4) & (j == heads + h)) \
        - ((l == 5) & (j == 2 * heads + h))
    return jnp.asarray(sel_q, BF16), jnp.asarray(sel_k, BF16)


def _fox_prep(h_main, hf, bf_row, qg_row, kg_row, *, batch, seq, heads, dh, rows):
    m = h_main.shape[0]
    w = heads * dh
    nt = seq // rows
    sel_q, sel_k = _fox_bias_selectors(heads, dh)

    def col(j):
        return pl.BlockSpec((rows, w), lambda b, t: (b * nt + t, j))

    def const(r, n):
        return pl.BlockSpec((r, n), lambda b, t: (0, 0))

    aug = pl.BlockSpec((rows, heads * LANES), lambda b, t: (b * nt + t, 0))
    return pl.pallas_call(
        functools.partial(_fox_prep_kernel, rows=rows, heads=heads, dh=dh),
        grid=(batch, nt),
        in_specs=[col(0), col(1), col(2),
                  pl.BlockSpec((rows, LANES), lambda b, t: (b * nt + t, 0)),
                  const(1, LANES), const(1, w), const(1, w),
                  const(LANES, heads * LANES), const(LANES, heads * LANES)],
        out_specs=[aug, aug,
                   pl.BlockSpec((1, heads, FOX_VROWS, rows), lambda b, t: (b, 0, 0, t)),
                   pl.BlockSpec((1, 8, LANES), lambda b, t: (b * nt + t, 0, 0))],
        out_shape=[jax.ShapeDtypeStruct((m, heads * LANES), BF16),
                   jax.ShapeDtypeStruct((m, heads * LANES), BF16),
                   jax.ShapeDtypeStruct((batch, heads, FOX_VROWS, seq), BF16),
                   jax.ShapeDtypeStruct((batch * nt, 8, LANES), F32)],
        scratch_shapes=[pltpu.VMEM((1, LANES), F32)],
        compiler_params=_params("parallel", "arbitrary"),
        name="fox_prep",
    )(h_main, h_main, h_main, hf, bf_row, qg_row, kg_row, sel_q, sel_k)


def _fox_attn_kernel(tq_ref, tk_ref, steps_ref, q_ref, k_ref, v_ref, z_ref, o_ref, s0_ref, s1_ref, m_ref, acc_ref,
                     *, tile, dh, per_block, nq, unroll, heads, tasks):
    half = tile // 2
    below_half = _iota2((half, half), 0) <= _iota2((half, half), 1)

    def logits(a, qi, kj, s_ref):
        qs = pl.multiple_of(jnp.minimum(qi, nq - 1) * tile, tile)
        ks = pl.multiple_of(kj * tile, tile)
        s_ref[a] = lax.dot_general(k_ref[pl.ds(ks, tile), a * LANES:(a + 1) * LANES],
                                   q_ref[pl.ds(qs, tile), a * LANES:(a + 1) * LANES], _NT,
                                   preferred_element_type=F32)

    def diagonal_logits(a, d, s_ref):
        ds = pl.multiple_of(d * tile, tile)
        lanes = slice(a * LANES, (a + 1) * LANES)
        s_ref[a, :half, :] = lax.dot_general(k_ref[pl.ds(ds, half), lanes], q_ref[pl.ds(ds, tile), lanes], _NT,
                                             preferred_element_type=F32)
        s_ref[a, half:, half:] = lax.dot_general(k_ref[pl.ds(ds + half, half), lanes],
                                                 q_ref[pl.ds(ds + half, half), lanes], _NT,
                                                 preferred_element_type=F32)

    def diagonal_softmax_pv(a, d, s_ref):
        ds = pl.multiple_of(d * tile, tile)
        early = jnp.where(below_half, s_ref[a, :half, :half], -jnp.inf)
        cross = s_ref[a, :half, half:]
        late = jnp.where(below_half, s_ref[a, half:, half:], -jnp.inf)
        m_early = jnp.max(early, axis=0, keepdims=True)
        m_late = jnp.maximum(jnp.max(cross, axis=0, keepdims=True), jnp.max(late, axis=0, keepdims=True))
        p_top = jnp.concatenate([jnp.exp2(early - m_early), jnp.exp2(cross - m_late)], axis=1).astype(BF16)
        p_late = jnp.exp2(late - m_late).astype(BF16)
        top = jnp.dot(v_ref[0, a, :, pl.ds(ds, half)], p_top, preferred_element_type=F32)
        bottom = jnp.dot(v_ref[0, a, :, pl.ds(ds + half, half)], p_late, preferred_element_type=F32)
        acc_ref[d, a, :, :half] = top[:, :half]
        acc_ref[d, a, :, half:] = top[:, half:] + bottom
        m_ref[d, a] = jnp.concatenate([m_early, m_late], axis=1)

    def softmax_pv(a, qi, kj, s_ref):
        ks = pl.multiple_of(kj * tile, tile)
        vt = v_ref[0, a, :, pl.ds(ks, tile)]
        s = s_ref[a]
        m_prev = m_ref[qi, a]
        m_new = jnp.maximum(m_prev, jnp.max(s, axis=0, keepdims=True))
        p = jnp.exp2(s - m_new).astype(BF16)
        acc_ref[qi, a] = jnp.exp2(m_prev - m_new) * acc_ref[qi, a] + jnp.dot(vt, p, preferred_element_type=F32)
        m_ref[qi, a] = m_new

    bufs = (s0_ref, s1_ref)
    everyone = range(per_block)
    m_ref[nq] = jnp.zeros(m_ref.shape[1:], F32)
    acc_ref[nq] = jnp.zeros(acc_ref.shape[1:], F32)

    for a in everyone:
        diagonal_logits(a, 0, s0_ref)

    def diagonal_steps(u, _):
        for r in range(unroll):
            d = unroll * u + r
            nxt = jnp.minimum(d + 1, nq - 1)
            for a in everyone:
                diagonal_logits(a, nxt, bufs[(r + 1) % 2])
            for a in everyone:
                diagonal_softmax_pv(a, d, bufs[r % 2])
        return 0

    lax.fori_loop(0, nq // unroll, diagonal_steps, 0)

    for a in everyone:
        head = pl.program_id(1) * per_block + a
        base = (pl.program_id(0) * heads + head) * tasks
        logits(a, tq_ref[base], tk_ref[base], s0_ref)

        def lower_steps(u, _, a=a, base=base):
            for r in range(unroll):
                t = base + unroll * u + r
                nxt = jnp.minimum(t + 1, base + tasks - 1)
                logits(a, tq_ref[nxt], tk_ref[nxt], bufs[(r + 1) % 2])
                softmax_pv(a, tq_ref[t], tk_ref[t], bufs[r % 2])
            return 0

        lax.fori_loop(0, steps_ref[pl.program_id(0) * heads + head], lower_steps, 0)

    def finish(qi, _):
        qs = pl.multiple_of(qi * tile, tile)
        o_t = jnp.concatenate([acc_ref[qi, a, :dh, :] / acc_ref[qi, a, dh:dh + 1, :] for a in everyone],
                              axis=0)
        o_ref[pl.ds(qs, tile), :] = (o_t.T * _silu(z_ref[pl.ds(qs, tile), :].astype(F32))).astype(o_ref.dtype)
        return 0

    lax.fori_loop(0, nq, finish, 0)


FOX_UNDERFLOW_LOG2 = 150.0
FOX_NORM_SLACK = 1.02
FOX_BIAS_SLACK = 2.0


def _fox_task_lists(edges, q_gain, k_gain, *, batch, heads, dh, nq, unroll):
    qk_bound = FOX_NORM_SLACK * dh ** 0.5 * LOG2E * jnp.max(jnp.abs(q_gain)) * jnp.max(jnp.abs(k_gain))
    threshold = 2.0 * qk_bound + FOX_UNDERFLOW_LOG2 + FOX_BIAS_SLACK
    edges = edges.reshape(batch, nq, 8, LANES)
    first = edges[:, :, 0, :heads].transpose(0, 2, 1)
    last = edges[:, :, 1, :heads].transpose(0, 2, 1)
    qi = jnp.arange(nq)[:, None]
    kj = jnp.arange(nq)[None, :]
    keep = (kj < qi) & (first[..., :, None] - last[..., None, :] >= -threshold)
    start = jnp.min(jnp.where(keep, kj, qi), axis=-1)
    length = jnp.arange(nq) - start
    ends = jnp.cumsum(length, axis=-1)
    count = ends[..., -1]
    tasks = nq * (nq - 1) // 2
    t = jnp.arange(tasks)
    row = jnp.sum(ends[..., None, :] <= t[:, None], axis=-1)
    in_row = row[..., None] == jnp.arange(nq)
    tk = t + jnp.sum(jnp.where(in_row, (start - ends + length)[..., None, :], 0), axis=-1)
    tq = jnp.where(t < count[..., None], row, nq).astype(jnp.int32)
    tk = jnp.where(t < count[..., None], tk, 0).astype(jnp.int32)
    steps = ((count + unroll - 1) // unroll).astype(jnp.int32)
    return tq.reshape(-1), tk.reshape(-1), steps.reshape(-1), tasks


def _fox_attn(qa, ka, vt, h_main, edges, q_gain, k_gain, *, batch, seq, heads, dh, tile, unroll):
    m = qa.shape[0]
    nq = seq // tile
    assert unroll % 2 == 0 and nq % unroll == 0 and (nq * (nq - 1) // 2) % unroll == 0
    per_block = LANES // dh
    blocks = heads // per_block
    z_off = 3 * blocks
    wide = per_block * LANES
    tq, tk, steps, tasks = _fox_task_lists(edges, q_gain, k_gain, batch=batch, heads=heads, dh=dh, nq=nq,
                                           unroll=unroll)
    grid_spec = pltpu.PrefetchScalarGridSpec(
        num_scalar_prefetch=3,
        grid=(batch, blocks),
        in_specs=[pl.BlockSpec((seq, wide), lambda b, h, *_: (b, h)),
                  pl.BlockSpec((seq, wide), lambda b, h, *_: (b, h)),
                  pl.BlockSpec((1, per_block, FOX_VROWS, seq), lambda b, h, *_: (b, h, 0, 0)),
                  pl.BlockSpec((seq, LANES), lambda b, h, *_: (b, z_off + h))],
        out_specs=pl.BlockSpec((seq, LANES), lambda b, h, *_: (b, h)),
        scratch_shapes=[pltpu.VMEM((per_block, tile, tile), F32), pltpu.VMEM((per_block, tile, tile), F32),
                        pltpu.VMEM((nq + 1, per_block, 1, tile), F32),
                        pltpu.VMEM((nq + 1, per_block, FOX_VROWS, tile), F32)])
    return pl.pallas_call(
        functools.partial(_fox_attn_kernel, tile=tile, dh=dh, per_block=per_block, nq=nq, unroll=unroll,
                          heads=heads, tasks=tasks),
        grid_spec=grid_spec,
        out_shape=jax.ShapeDtypeStruct((m, heads * dh), BF16),
        compiler_params=_params("parallel", "parallel"),
        name="fox_attn",
    )(tq, tk, steps, qa, ka, vt, h_main)


def _post_kernel(a_ref, x_ref, p_ref, wo_ref, g_ref, b_ref, wg_ref, wp_ref, o_ref, *, alpha, parts):
    sub = a_ref.shape[0] // parts
    blocks = [slice(r * sub, (r + 1) * sub) for r in range(parts)]
    proj = [jnp.dot(a_ref[rs, :], wo_ref[...], preferred_element_type=F32) for rs in blocks]
    emb = [_mm(p_ref[rs, :], wp_ref[...]) for rs in blocks]
    normed = []
    for rs, y in zip(blocks, proj):
        t = alpha * x_ref[rs, :] + y
        mu = jnp.mean(t, axis=-1, keepdims=True)
        d = t - mu
        var = jnp.mean(d * d, axis=-1, keepdims=True)
        normed.append(d * lax.rsqrt(var + LN_EPS) * g_ref[...] + b_ref[...])
    gates = [_mm(xn, wg_ref[...]) for xn in normed]
    for rs, xn, gate, e in zip(blocks, normed, gates, emb):
        o_ref[rs, :] = xn + _sigmoid(gate) * e


def _post(a, x, p, w_out, ln_g, ln_b, w_gate, w_proj, *, layer, mixer, alpha, tm):
    m, d = x.shape
    dp = p.shape[2]

    def rows(n):
        return pl.BlockSpec((tm, n), lambda i: (i, 0))

    def of_layer(index, r, n):
        return pl.BlockSpec((None, r, n), lambda i: (index, 0, 0))

    return pl.pallas_call(
        functools.partial(_post_kernel, alpha=alpha, parts=2),
        grid=(m // tm,),
        in_specs=[rows(a.shape[1]), rows(d), pl.BlockSpec((None, tm, dp), lambda i: (layer, i, 0)),
                  of_layer(mixer, a.shape[1], d), of_layer(layer, 1, d), of_layer(layer, 1, d),
                  of_layer(layer, d, d), of_layer(layer, dp, d)],
        out_specs=rows(d),
        out_shape=jax.ShapeDtypeStruct((m, d), F32),
        compiler_params=_params("parallel"),
        name="post_block",
    )(a, x, p, w_out, ln_g, ln_b, w_gate, w_proj)


def _pad_cols(w, n):
    return jnp.pad(w, ((0, 0), (0, n - w.shape[1])))


def _pad_row(v, n):
    return jnp.pad(v, (0, n - v.shape[0])).reshape(1, n)


def _gdn_layer(x, w_stack, layer, w_in, conv_w, a_log, dt_bias, norm_g, *, batch, seq):
    heads = a_log.shape[0]
    dk = norm_g.shape[0]
    wide = 4 * heads * dk
    w_gates = jnp.concatenate([_pad_cols(w_in[:, wide:wide + heads], LANES),
                               _pad_cols(w_in[:, wide + heads:], LANES)], axis=1).astype(BF16)
    h_main, hs = _in_proj(x, w_stack, layer, wide, w_gates, tm=512, tn=1024, name="gdn_in_proj")
    beta, gcum, gcum_t = _gdn_gates(hs, _pad_row(a_log, LANES), _pad_row(dt_bias, LANES), rows=512)
    return _gdn_core(h_main, conv_w, beta, gcum, gcum_t, norm_g.reshape(1, dk),
                     batch=batch, seq=seq, heads=heads, dk=dk, rows=512, hb=4)


def _fox_layer(x, w_stack, layer, w_in, b_f, q_norm_g, k_norm_g, *, batch, seq):
    heads = b_f.shape[0]
    dh = q_norm_g.shape[0]
    wide = 4 * heads * dh
    h_main, hf = _in_proj(x, w_stack, layer, wide, _pad_cols(w_in[:, wide:], LANES).astype(BF16),
                          tm=512, tn=1024, name="fox_in_proj")
    qa, ka, vt, edges = _fox_prep(h_main, hf, _pad_row(b_f, LANES),
                                  jnp.tile(q_norm_g, heads).reshape(1, heads * dh),
                                  jnp.tile(k_norm_g, heads).reshape(1, heads * dh),
                                  batch=batch, seq=seq, heads=heads, dh=dh, rows=FOX_TILE)
    return _fox_attn(qa, ka, vt, h_main, edges, q_norm_g, k_norm_g, batch=batch, seq=seq, heads=heads, dh=dh,
                     tile=FOX_TILE, unroll=8)


def kernel(x, p, ln_g, ln_b, ple_w_gate, ple_w_proj, gdn_w_in, gdn_conv_w, gdn_a_log, gdn_dt_bias,
           gdn_norm_g, gdn_w_out, fox_w_in, fox_b_f, fox_q_norm_g, fox_k_norm_g, fox_w_out):
    batch, seq, d = x.shape
    depth = ln_g.shape[0]
    alpha = (2 * depth) ** 0.25
    xf = x.reshape(batch * seq, d)
    p_rows = p.reshape(depth, batch * seq, -1)
    ln_g3, ln_b3 = ln_g.reshape(depth, 1, d), ln_b.reshape(depth, 1, d)
    w_gate, w_proj = ple_w_gate.astype(BF16), ple_w_proj.astype(BF16)
    gdn_w, fox_w = gdn_w_in.astype(BF16), fox_w_in.astype(BF16)
    gdn_wo, fox_wo = gdn_w_out.astype(BF16), fox_w_out.astype(BF16)
    for i in range(depth):
        j = i // 2
        if i % 2 == 0:
            a = _gdn_layer(xf, gdn_w, j, gdn_w_in[j], gdn_conv_w[j], gdn_a_log[j], gdn_dt_bias[j], gdn_norm_g[j],
                           batch=batch, seq=seq)
            w_out = gdn_wo
        else:
            a = _fox_layer(xf, fox_w, j, fox_w_in[j], fox_b_f[j], fox_q_norm_g[j], fox_k_norm_g[j],
                           batch=batch, seq=seq)
            w_out = fox_wo
        xf = _post(a, xf, p_rows, w_out, ln_g3, ln_b3, w_gate, w_proj, layer=i, mixer=j, alpha=alpha, tm=512)
    return xf.reshape(batch, seq, d)
```

```python
import functools

import jax
import jax.numpy as jnp
import numpy as np
from jax import lax
from jax.experimental import pallas as pl
from jax.experimental.pallas import tpu as pltpu

F32 = jnp.float32
BF16 = jnp.bfloat16

LANES = 128
V7X_VMEM_BYTES = 64 * 1024 * 1024
VMEM_LIMIT = V7X_VMEM_BYTES * 3 // 4

GDN_CHUNK = 64
LN_EPS = 1e-5
RMS_EPS = 1e-6
LOG2E = 1.4426950408889634

_NT = (((1,), (1,)), ((), ()))
_TN = (((0,), (0,)), ((), ()))


def _mm(a, b):
    return jnp.dot(a.astype(BF16), b.astype(BF16), preferred_element_type=F32)


def _sigmoid(x):
    return 1.0 / (1.0 + jnp.exp(-x))


def _silu(x):
    h = 0.5 * x
    return h + h * jnp.tanh(h)


def _softplus(x):
    return jnp.maximum(x, 0.0) + jnp.log1p(jnp.exp(-jnp.abs(x)))


def _split3(x):
    hi = x.astype(BF16).astype(F32)
    r = x - hi
    mid = r.astype(BF16).astype(F32)
    lo = (r - mid).astype(BF16).astype(F32)
    return hi, mid, lo


def _iota2(shape, dim):
    return lax.broadcasted_iota(jnp.int32, shape, dim)


def _params(*semantics):
    return pltpu.CompilerParams(dimension_semantics=semantics, vmem_limit_bytes=VMEM_LIMIT)


def _in_proj_kernel(x_ref, wm_ref, ws_ref, hm_ref, hs_ref, *, tn):
    xb = x_ref[...].astype(BF16)
    for j in range(wm_ref.shape[1] // tn):
        cols = slice(j * tn, (j + 1) * tn)
        hm_ref[:, cols] = jnp.dot(xb, wm_ref[:, cols], preferred_element_type=F32).astype(hm_ref.dtype)
    hs_ref[...] = jnp.dot(xb, ws_ref[...], preferred_element_type=F32)


def _in_proj(x, w_stack, layer, n, w_small, *, tm, tn, name):
    m, k = x.shape
    ns = w_small.shape[1]
    return pl.pallas_call(
        functools.partial(_in_proj_kernel, tn=tn),
        grid=(m // tm,),
        in_specs=[pl.BlockSpec((tm, k), lambda i: (i, 0)),
                  pl.BlockSpec((None, k, n), lambda i: (layer, 0, 0)),
                  pl.BlockSpec((k, ns), lambda i: (0, 0))],
        out_specs=[pl.BlockSpec((tm, n), lambda i: (i, 0)),
                   pl.BlockSpec((tm, ns), lambda i: (i, 0))],
        out_shape=[jax.ShapeDtypeStruct((m, n), BF16), jax.ShapeDtypeStruct((m, ns), F32)],
        compiler_params=_params("parallel"),
        name=name,
    )(x, w_stack, w_small)


def _gdn_gate_kernel(hs_ref, alog_ref, dtb_ref, beta_ref, gcum_ref, gcum_t_ref, *, rows):
    c = GDN_CHUNK
    beta_ref[...] = _sigmoid(hs_ref[:, :LANES])
    g = -jnp.exp(alog_ref[...]) * _softplus(hs_ref[:, LANES:] + dtb_ref[...])
    g3 = jnp.concatenate([part.astype(BF16) for part in _split3(g)], axis=1)
    tril = (_iota2((c, c), 0) >= _iota2((c, c), 1)).astype(BF16)
    eye = (_iota2((8, LANES), 0) == _iota2((8, LANES), 1)).astype(BF16)
    for i in range(rows // c):
        sums = jnp.dot(tril, g3[i * c:(i + 1) * c], preferred_element_type=F32)
        gc = sums[:, :LANES] + sums[:, LANES:2 * LANES] + sums[:, 2 * LANES:]
        gcum_ref[i * c:(i + 1) * c, :] = gc
        hi, mid, lo = (lax.dot_general(eye, part.astype(BF16), _NT, preferred_element_type=F32)
                       for part in _split3(gc))
        gcum_t_ref[i] = hi + mid + lo


def _gdn_gates(hs, alog_row, dtb_row, *, rows):
    m = hs.shape[0]
    c = GDN_CHUNK
    return pl.pallas_call(
        functools.partial(_gdn_gate_kernel, rows=rows),
        grid=(m // rows,),
        in_specs=[pl.BlockSpec((rows, 2 * LANES), lambda i: (i, 0)),
                  pl.BlockSpec((1, LANES), lambda i: (0, 0)),
                  pl.BlockSpec((1, LANES), lambda i: (0, 0))],
        out_specs=[pl.BlockSpec((rows, LANES), lambda i: (i, 0)),
                   pl.BlockSpec((rows, LANES), lambda i: (i, 0)),
                   pl.BlockSpec((rows // c, 8, c), lambda i: (i, 0, 0))],
        out_shape=[jax.ShapeDtypeStruct((m, LANES), F32),
                   jax.ShapeDtypeStruct((m, LANES), F32),
                   jax.ShapeDtypeStruct((m // c, 8, c), F32)],
        compiler_params=_params("parallel"),
        name="gdn_gates",
    )(hs, alog_row, dtb_row)


def _gdn_core_kernel(hq_ref, hk_ref, hv_ref, hz_ref, wq_ref, wk_ref, wv_ref, beta_ref, gcum_ref,
                     gcum_t_ref, ng_ref, o_ref, state_ref, xe_ref, *, rows, dk, hb):
    c = GDN_CHUNK
    n = rows // c
    group = pl.program_id(1)

    @pl.when(pl.program_id(2) == 0)
    def _():
        state_ref[...] = jnp.zeros_like(state_ref)
        xe_ref[:, :8, :] = jnp.zeros((3, 8, xe_ref.shape[2]), F32)

    def conv_silu(x_ref, w_ref, slot):
        w = w_ref[...]
        taps = w.shape[0]
        xe_ref[slot, 8:, :] = x_ref[...].astype(F32)
        y = w[taps - 1:taps, :] * xe_ref[slot, 8:, :]
        for t in range(taps - 1):
            y = y + w[t:t + 1, :] * xe_ref[slot, pl.ds(8 - taps + 1 + t, rows), :]
        xe_ref[slot, :8, :] = xe_ref[slot, rows:, :]
        return _silu(y)

    def l2norm(x):
        return x * lax.rsqrt(jnp.sum(x * x, axis=-1, keepdims=True) + RMS_EPS)

    k_all = conv_silu(hk_ref, wk_ref, 1)
    q_all = conv_silu(hq_ref, wq_ref, 0)
    v_all = conv_silu(hv_ref, wv_ref, 2)
    lane = _iota2((rows, LANES), 1)
    ri = _iota2((c, c), 0)
    ci = _iota2((c, c), 1)
    causal = ri >= ci
    strict = ri > ci
    eye = (ri == ci).astype(F32)
    dv = dk

    units = []
    for a in range(hb):
        head = group * hb + a
        cols = slice(a * dk, (a + 1) * dk)
        k = l2norm(k_all[:, cols])
        q = l2norm(q_all[:, cols]) * (dk ** -0.5)
        v = v_all[:, cols]
        beta = jnp.sum(jnp.where(lane == head, beta_ref[...], 0.0), axis=1, keepdims=True)
        gcol = jnp.sum(jnp.where(lane == head, gcum_ref[...], 0.0), axis=1, keepdims=True)
        for i in range(n):
            sl = slice(i * c, (i + 1) * c)
            units.append(dict(a=a, q=q[sl], k=k[sl], v=v[sl], beta=beta[sl], g=gcol[sl],
                              grow=gcum_t_ref[i, pl.ds(head, 1), :]))
    for u in units:
        u["glast"] = u["g"][c - 1:c, :]
        u["eg"] = jnp.exp(u["g"])
        u["decay"] = jnp.where(causal, jnp.exp(u["g"] - u["grow"]), 0.0)
        u["kb"] = u["k"] * u["beta"]
    for u in units:
        kbf = u["k"].astype(BF16)
        u["kk"] = lax.dot_general(u["kb"].astype(BF16), kbf, _NT, preferred_element_type=F32)
        u["qk"] = lax.dot_general(u["q"].astype(BF16), kbf, _NT, preferred_element_type=F32)
    for u in units:
        u["lmat"] = jnp.where(strict, u["kk"] * u["decay"], 0.0)
        u["amat"] = jnp.where(causal, u["qk"] * u["decay"], 0.0)
        u["tinv"] = eye - u["lmat"]
    for u in units:
        u["pk"] = _mm(u["lmat"], u["lmat"])
    span = 4
    while span < c:
        for u in units:
            u["both"] = _mm(u["pk"], jnp.concatenate([u["tinv"], u["pk"]], axis=1))
        for u in units:
            u["tinv"] = u["tinv"] + u["both"][:, :c]
            u["pk"] = u["both"][:, c:]
        span *= 2
    for u in units:
        u["tinv"] = u["tinv"] + _mm(u["pk"], u["tinv"])
    for u in units:
        u["x"] = _mm(u["tinv"], jnp.concatenate([u["v"] * u["beta"], u["kb"] * u["eg"]], axis=1))
    for u in units:
        u["ax"] = _mm(u["amat"], u["x"])
        kd = u["k"] * jnp.exp(u["glast"] - u["g"])
        u["kx"] = lax.dot_general(kd.astype(BF16), u["x"].astype(BF16), _TN, preferred_element_type=F32)
    for u in units:
        u["qeff"] = u["q"] * u["eg"] - u["ax"][:, dv:]
    states = [state_ref[a] for a in range(hb)]
    outs = [[] for _ in range(hb)]
    for i in range(n):
        for a in range(hb):
            u = units[a * n + i]
            sb = states[a].astype(BF16)
            outs[a].append(_mm(u["qeff"], sb) + u["ax"][:, :dv])
            states[a] = jnp.exp(u["glast"]) * states[a] + u["kx"][:, :dv] - _mm(u["kx"][:, dv:], sb)
    for a in range(hb):
        state_ref[a] = states[a]
        o = jnp.concatenate(outs[a], axis=0)
        o = o * lax.rsqrt(jnp.mean(o * o, axis=-1, keepdims=True) + RMS_EPS) * ng_ref[...]
        cols = slice(a * dk, (a + 1) * dk)
        o_ref[:, cols] = (o * _silu(hz_ref[:, cols].astype(F32))).astype(o_ref.dtype)


def _gdn_core(h_main, conv_w, beta, gcum, gcum_t, ng_row, *, batch, seq, heads, dk, rows, hb):
    m = h_main.shape[0]
    nb = seq // rows
    c = GDN_CHUNK
    taps = conv_w.shape[0]
    groups = heads // hb
    wide = hb * dk

    def col(off):
        return pl.BlockSpec((rows, wide), lambda b, h, j: (b * nb + j, off + h))

    def wcol(off):
        return pl.BlockSpec((taps, wide), lambda b, h, j: (0, off + h))

    def gate():
        return pl.BlockSpec((rows, LANES), lambda b, h, j: (b * nb + j, 0))

    return pl.pallas_call(
        functools.partial(_gdn_core_kernel, rows=rows, dk=dk, hb=hb),
        grid=(batch, groups, nb),
        in_specs=[col(0), col(groups), col(2 * groups), col(3 * groups),
                  wcol(0), wcol(groups), wcol(2 * groups),
                  gate(), gate(),
                  pl.BlockSpec((rows // c, 8, c), lambda b, h, j: (b * nb + j, 0, 0)),
                  pl.BlockSpec((1, dk), lambda b, h, j: (0, 0))],
        out_specs=pl.BlockSpec((rows, wide), lambda b, h, j: (b * nb + j, h)),
        out_shape=jax.ShapeDtypeStruct((m, heads * dk), BF16),
        scratch_shapes=[pltpu.VMEM((hb, dk, dk), F32), pltpu.VMEM((3, rows + 8, wide), F32)],
        compiler_params=_params("parallel", "parallel", "arbitrary"),
        name="gdn_core",
    )(h_main, h_main, h_main, h_main, conv_w, conv_w, conv_w, beta, gcum, gcum_t, ng_row)


FOX_TILE = 512
FOX_VROWS = 80


def _fox_prep_kernel(hq_ref, hk_ref, hv_ref, hf_ref, bf_ref, qg_ref, kg_ref, selq_ref, selk_ref,
                     qa_ref, ka_ref, vt_ref, edge_ref, carry_ref, *, rows, heads, dh):
    @pl.when(pl.program_id(1) == 0)
    def _():
        carry_ref[...] = jnp.zeros_like(carry_ref)

    logf = -_softplus(-(hf_ref[...] + bf_ref[...]))
    lane = _iota2((rows, LANES), 1)
    logf = jnp.where(lane < heads, logf, 0.0)
    tril = (_iota2((rows, rows), 0) >= _iota2((rows, rows), 1)).astype(BF16)
    sums = jnp.dot(tril, jnp.concatenate([part.astype(BF16) for part in _split3(logf)], axis=1),
                   preferred_element_type=F32)
    c = sums[:, :LANES] + sums[:, LANES:2 * LANES] + sums[:, 2 * LANES:] + carry_ref[...]
    carry_ref[...] = c[rows - 1:rows, :]
    c2 = c * LOG2E
    edge_ref[0] = jnp.concatenate([c2[:1], c2[rows - 1:], jnp.zeros((6, LANES), F32)], axis=0)
    chi, cmid, clo = _split3(c2)
    parts = (chi + pltpu.roll(cmid, heads, axis=1) + pltpu.roll(clo, 2 * heads, axis=1)
             + jnp.where(lane == 3 * heads, 1.0, 0.0)).astype(BF16)
    q_tails = jnp.dot(parts, selq_ref[...], preferred_element_type=F32)
    k_tails = jnp.dot(parts, selk_ref[...], preferred_element_type=F32)

    same_head = _iota2((LANES, LANES), 0) // dh == _iota2((LANES, LANES), 1) // dh
    head_mean = jnp.where(same_head, 1.0 / dh, 0.0).astype(BF16)

    def head_rms(x, g, scale):
        mean_sq = jnp.dot((x * x).astype(BF16), head_mean, preferred_element_type=F32)
        return x * lax.rsqrt(mean_sq + RMS_EPS) * (g * scale)

    per_block = LANES // dh
    vrow = _iota2((FOX_VROWS, LANES), 0)
    vlane = _iota2((FOX_VROWS, LANES), 1)
    out_row = _iota2((FOX_VROWS, rows), 0)
    for j in range(heads // per_block):
        cols = slice(j * LANES, (j + 1) * LANES)
        qn = head_rms(hq_ref[:, cols].astype(F32), qg_ref[:, cols], dh ** -0.5 * LOG2E)
        kn = head_rms(hk_ref[:, cols].astype(F32), kg_ref[:, cols], 1.0)
        vb = hv_ref[:, cols].astype(BF16)
        for a in range(per_block):
            h = j * per_block + a
            qh = qn if a == 0 else pltpu.roll(qn, LANES - a * dh, axis=1)
            kh = kn if a == 0 else pltpu.roll(kn, LANES - a * dh, axis=1)
            out_cols = slice(h * LANES, (h + 1) * LANES)
            qa_ref[:, out_cols] = jnp.where(lane < dh, qh, q_tails[:, out_cols]).astype(qa_ref.dtype)
            ka_ref[:, out_cols] = jnp.where(lane < dh, kh, k_tails[:, out_cols]).astype(ka_ref.dtype)
            pick = ((vlane == a * dh + vrow) & (vrow < dh)).astype(BF16)
            vt = lax.dot_general(pick, vb, _NT, preferred_element_type=F32)
            vt_ref[0, h] = jnp.where(out_row == dh, 1.0, vt).astype(vt_ref.dtype)


def _fox_bias_selectors(heads, dh):
    j = np.arange(LANES)[:, None]
    col = np.arange(heads * LANES)[None, :]
    h, l = col // LANES, col % LANES - dh
    part = lambda i: (l == i) & (j == i * heads + h)
    ones = lambda lo: (l >= lo) & (l < lo + 3) & (j == 3 * heads)
    sel_q = part(0) | part(1) | part(2) | ones(3)
    sel_k = ones(0).astype(np.float32) - ((l == 3) & (j == h)) - ((l == 4) & (j == heads + h)) \
        - ((l == 5) & (j == 2 * heads + h))
    return jnp.asarray(sel_q, BF16), jnp.asarray(sel_k, BF16)


def _fox_prep(h_main, hf, bf_row, qg_row, kg_row, *, batch, seq, heads, dh, rows):
    m = h_main.shape[0]
    w = heads * dh
    nt = seq // rows
    sel_q, sel_k = _fox_bias_selectors(heads, dh)

    def col(j):
        return pl.BlockSpec((rows, w), lambda b, t: (b * nt + t, j))

    def const(r, n):
        return pl.BlockSpec((r, n), lambda b, t: (0, 0))

    aug = pl.BlockSpec((rows, heads * LANES), lambda b, t: (b * nt + t, 0))
    return pl.pallas_call(
        functools.partial(_fox_prep_kernel, rows=rows, heads=heads, dh=dh),
        grid=(batch, nt),
        in_specs=[col(0), col(1), col(2),
                  pl.BlockSpec((rows, LANES), lambda b, t: (b * nt + t, 0)),
                  const(1, LANES), const(1, w), const(1, w),
                  const(LANES, heads * LANES), const(LANES, heads * LANES)],
        out_specs=[aug, aug,
                   pl.BlockSpec((1, heads, FOX_VROWS, rows), lambda b, t: (b, 0, 0, t)),
                   pl.BlockSpec((1, 8, LANES), lambda b, t: (b * nt + t, 0, 0))],
        out_shape=[jax.ShapeDtypeStruct((m, heads * LANES), BF16),
                   jax.ShapeDtypeStruct((m, heads * LANES), BF16),
                   jax.ShapeDtypeStruct((batch, heads, FOX_VROWS, seq), BF16),
                   jax.ShapeDtypeStruct((batch * nt, 8, LANES), F32)],
        scratch_shapes=[pltpu.VMEM((1, LANES), F32)],
        compiler_params=_params("parallel", "arbitrary"),
        name="fox_prep",
    )(h_main, h_main, h_main, hf, bf_row, qg_row, kg_row, sel_q, sel_k)


def _fox_attn_kernel(tq_ref, tk_ref, count_ref, q_ref, k_ref, v_ref, z_ref, o_ref, s0_ref, s1_ref, m_ref, acc_ref,
                     *, tile, dh, per_block, nq, unroll, heads, tasks):
    half = tile // 2
    below_half = _iota2((half, half), 0) <= _iota2((half, half), 1)

    def logits(a, qi, kj, s_ref):
        qs = pl.multiple_of(jnp.minimum(qi, nq - 1) * tile, tile)
        ks = pl.multiple_of(kj * tile, tile)
        s_ref[a] = lax.dot_general(k_ref[pl.ds(ks, tile), a * LANES:(a + 1) * LANES],
                                   q_ref[pl.ds(qs, tile), a * LANES:(a + 1) * LANES], _NT,
                                   preferred_element_type=F32)

    def diagonal_logits(a, d, s_ref):
        ds = pl.multiple_of(d * tile, tile)
        lanes = slice(a * LANES, (a + 1) * LANES)
        s_ref[a, :half, :] = lax.dot_general(k_ref[pl.ds(ds, half), lanes], q_ref[pl.ds(ds, tile), lanes], _NT,
                                             preferred_element_type=F32)
        s_ref[a, half:, half:] = lax.dot_general(k_ref[pl.ds(ds + half, half), lanes],
                                                 q_ref[pl.ds(ds + half, half), lanes], _NT,
                                                 preferred_element_type=F32)

    def diagonal_softmax_pv(a, d, s_ref):
        ds = pl.multiple_of(d * tile, tile)
        early = jnp.where(below_half, s_ref[a, :half, :half], -jnp.inf)
        cross = s_ref[a, :half, half:]
        late = jnp.where(below_half, s_ref[a, half:, half:], -jnp.inf)
        m_early = jnp.max(early, axis=0, keepdims=True)
        m_late = jnp.maximum(jnp.max(cross, axis=0, keepdims=True), jnp.max(late, axis=0, keepdims=True))
        p_top = jnp.concatenate([jnp.exp2(early - m_early), jnp.exp2(cross - m_late)], axis=1).astype(BF16)
        p_late = jnp.exp2(late - m_late).astype(BF16)
        top = jnp.dot(v_ref[0, a, :, pl.ds(ds, half)], p_top, preferred_element_type=F32)
        bottom = jnp.dot(v_ref[0, a, :, pl.ds(ds + half, half)], p_late, preferred_element_type=F32)
        acc_ref[d, a, :, :half] = top[:, :half]
        acc_ref[d, a, :, half:] = top[:, half:] + bottom
        m_ref[d, a] = jnp.concatenate([m_early, m_late], axis=1)

    def softmax_pv(a, qi, kj, s_ref):
        ks = pl.multiple_of(kj * tile, tile)
        vt = v_ref[0, a, :, pl.ds(ks, tile)]
        s = s_ref[a]
        m_prev = m_ref[qi, a]
        m_new = jnp.maximum(m_prev, jnp.max(s, axis=0, keepdims=True))
        p = jnp.exp2(s - m_new).astype(BF16)
        acc_ref[qi, a] = jnp.exp2(m_prev - m_new) * acc_ref[qi, a] + jnp.dot(vt, p, preferred_element_type=F32)
        m_ref[qi, a] = m_new

    bufs = (s0_ref, s1_ref)
    everyone = range(per_block)
    m_ref[nq] = jnp.zeros(m_ref.shape[1:], F32)
    acc_ref[nq] = jnp.zeros(acc_ref.shape[1:], F32)

    for a in everyone:
        diagonal_logits(a, 0, s0_ref)

    def diagonal_steps(u, _):
        for r in range(unroll):
            d = unroll * u + r
            nxt = jnp.minimum(d + 1, nq - 1)
            for a in everyone:
                diagonal_logits(a, nxt, bufs[(r + 1) % 2])
            for a in everyone:
                diagonal_softmax_pv(a, d, bufs[r % 2])
        return 0

    lax.fori_loop(0, nq // unroll, diagonal_steps, 0)

    for a in everyone:
        head = pl.program_id(1) * per_block + a
        base = (pl.program_id(0) * heads + head) * tasks
        logits(a, tq_ref[base], tk_ref[base], s0_ref)

        def lower_steps(count, first, a=a, base=base):
            def run(u, _):
                for r in range(count):
                    t = base + first + count * u + r
                    nxt = jnp.minimum(t + 1, base + tasks - 1)
                    logits(a, tq_ref[nxt], tk_ref[nxt], bufs[(r + 1) % 2])
                    softmax_pv(a, tq_ref[t], tk_ref[t], bufs[r % 2])
                return 0
            return run

        kept = count_ref[pl.program_id(0) * heads + head]
        main = kept // unroll
        lax.fori_loop(0, main, lower_steps(unroll, 0), 0)
        lax.fori_loop(0, (kept - main * unroll + 1) // 2, lower_steps(2, main * unroll), 0)

    def finish(qi, _):
        qs = pl.multiple_of(qi * tile, tile)
        o_t = jnp.concatenate([acc_ref[qi, a, :dh, :] / acc_ref[qi, a, dh:dh + 1, :] for a in everyone],
                              axis=0)
        o_ref[pl.ds(qs, tile), :] = (o_t.T * _silu(z_ref[pl.ds(qs, tile), :].astype(F32))).astype(o_ref.dtype)
        return 0

    lax.fori_loop(0, nq, finish, 0)


FOX_UNDERFLOW_LOG2 = 150.0
FOX_NORM_SLACK = 1.02
FOX_BIAS_SLACK = 2.0


def _fox_task_lists(edges, q_gain, k_gain, *, batch, heads, dh, nq):
    qk_bound = FOX_NORM_SLACK * dh ** 0.5 * LOG2E * jnp.max(jnp.abs(q_gain)) * jnp.max(jnp.abs(k_gain))
    threshold = 2.0 * qk_bound + FOX_UNDERFLOW_LOG2 + FOX_BIAS_SLACK
    edges = edges.reshape(batch, nq, 8, LANES)
    first = edges[:, :, 0, :heads].transpose(0, 2, 1)
    last = edges[:, :, 1, :heads].transpose(0, 2, 1)
    qi = jnp.arange(nq)[:, None]
    kj = jnp.arange(nq)[None, :]
    keep = (kj < qi) & (first[..., :, None] - last[..., None, :] >= -threshold)
    start = jnp.min(jnp.where(keep, kj, qi), axis=-1)
    length = jnp.arange(nq) - start
    ends = jnp.cumsum(length, axis=-1)
    count = ends[..., -1]
    tasks = nq * (nq - 1) // 2
    t = jnp.arange(tasks)
    row = jnp.sum(ends[..., None, :] <= t[:, None], axis=-1)
    in_row = row[..., None] == jnp.arange(nq)
    tk = t + jnp.sum(jnp.where(in_row, (start - ends + length)[..., None, :], 0), axis=-1)
    tq = jnp.where(t < count[..., None], row, nq).astype(jnp.int32)
    tk = jnp.where(t < count[..., None], tk, 0).astype(jnp.int32)
    return tq.reshape(-1), tk.reshape(-1), count.astype(jnp.int32).reshape(-1), tasks


def _fox_attn(qa, ka, vt, h_main, edges, q_gain, k_gain, *, batch, seq, heads, dh, tile, unroll):
    m = qa.shape[0]
    nq = seq // tile
    assert unroll % 2 == 0 and nq % unroll == 0 and (nq * (nq - 1) // 2) % 2 == 0
    per_block = LANES // dh
    blocks = heads // per_block
    z_off = 3 * blocks
    wide = per_block * LANES
    tq, tk, count, tasks = _fox_task_lists(edges, q_gain, k_gain, batch=batch, heads=heads, dh=dh, nq=nq)
    grid_spec = pltpu.PrefetchScalarGridSpec(
        num_scalar_prefetch=3,
        grid=(batch, blocks),
        in_specs=[pl.BlockSpec((seq, wide), lambda b, h, *_: (b, h)),
                  pl.BlockSpec((seq, wide), lambda b, h, *_: (b, h)),
                  pl.BlockSpec((1, per_block, FOX_VROWS, seq), lambda b, h, *_: (b, h, 0, 0)),
                  pl.BlockSpec((seq, LANES), lambda b, h, *_: (b, z_off + h))],
        out_specs=pl.BlockSpec((seq, LANES), lambda b, h, *_: (b, h)),
        scratch_shapes=[pltpu.VMEM((per_block, tile, tile), F32), pltpu.VMEM((per_block, tile, tile), F32),
                        pltpu.VMEM((nq + 1, per_block, 1, tile), F32),
                        pltpu.VMEM((nq + 1, per_block, FOX_VROWS, tile), F32)])
    return pl.pallas_call(
        functools.partial(_fox_attn_kernel, tile=tile, dh=dh, per_block=per_block, nq=nq, unroll=unroll,
                          heads=heads, tasks=tasks),
        grid_spec=grid_spec,
        out_shape=jax.ShapeDtypeStruct((m, heads * dh), BF16),
        compiler_params=_params("parallel", "parallel"),
        name="fox_attn",
    )(tq, tk, count, qa, ka, vt, h_main)


def _post_kernel(a_ref, x_ref, p_ref, wo_ref, g_ref, b_ref, wg_ref, wp_ref, o_ref, *, alpha, parts):
    sub = a_ref.shape[0] // parts
    blocks = [slice(r * sub, (r + 1) * sub) for r in range(parts)]
    proj = [jnp.dot(a_ref[rs, :], wo_ref[...], preferred_element_type=F32) for rs in blocks]
    emb = [_mm(p_ref[rs, :], wp_ref[...]) for rs in blocks]
    normed = []
    for rs, y in zip(blocks, proj):
        t = alpha * x_ref[rs, :] + y
        mu = jnp.mean(t, axis=-1, keepdims=True)
        d = t - mu
        var = jnp.mean(d * d, axis=-1, keepdims=True)
        normed.append(d * lax.rsqrt(var + LN_EPS) * g_ref[...] + b_ref[...])
    gates = [_mm(xn, wg_ref[...]) for xn in normed]
    for rs, xn, gate, e in zip(blocks, normed, gates, emb):
        o_ref[rs, :] = xn + _sigmoid(gate) * e


def _post(a, x, p, w_out, ln_g, ln_b, w_gate, w_proj, *, layer, mixer, alpha, tm):
    m, d = x.shape
    dp = p.shape[2]

    def rows(n):
        return pl.BlockSpec((tm, n), lambda i: (i, 0))

    def of_layer(index, r, n):
        return pl.BlockSpec((None, r, n), lambda i: (index, 0, 0))

    return pl.pallas_call(
        functools.partial(_post_kernel, alpha=alpha, parts=2),
        grid=(m // tm,),
        in_specs=[rows(a.shape[1]), rows(d), pl.BlockSpec((None, tm, dp), lambda i: (layer, i, 0)),
                  of_layer(mixer, a.shape[1], d), of_layer(layer, 1, d), of_layer(layer, 1, d),
                  of_layer(layer, d, d), of_layer(layer, dp, d)],
        out_specs=rows(d),
        out_shape=jax.ShapeDtypeStruct((m, d), F32),
        compiler_params=_params("parallel"),
        name="post_block",
    )(a, x, p, w_out, ln_g, ln_b, w_gate, w_proj)


def _pad_cols(w, n):
    return jnp.pad(w, ((0, 0), (0, n - w.shape[1])))


def _pad_row(v, n):
    return jnp.pad(v, (0, n - v.shape[0])).reshape(1, n)


def _gdn_layer(x, w_stack, layer, w_in, conv_w, a_log, dt_bias, norm_g, *, batch, seq):
    heads = a_log.shape[0]
    dk = norm_g.shape[0]
    wide = 4 * heads * dk
    w_gates = jnp.concatenate([_pad_cols(w_in[:, wide:wide + heads], LANES),
                               _pad_cols(w_in[:, wide + heads:], LANES)], axis=1).astype(BF16)
    h_main, hs = _in_proj(x, w_stack, layer, wide, w_gates, tm=512, tn=1024, name="gdn_in_proj")
    beta, gcum, gcum_t = _gdn_gates(hs, _pad_row(a_log, LANES), _pad_row(dt_bias, LANES), rows=512)
    return _gdn_core(h_main, conv_w, beta, gcum, gcum_t, norm_g.reshape(1, dk),
                     batch=batch, seq=seq, heads=heads, dk=dk, rows=512, hb=4)


def _fox_layer(x, w_stack, layer, w_in, b_f, q_norm_g, k_norm_g, *, batch, seq):
    heads = b_f.shape[0]
    dh = q_norm_g.shape[0]
    wide = 4 * heads * dh
    h_main, hf = _in_proj(x, w_stack, layer, wide, _pad_cols(w_in[:, wide:], LANES).astype(BF16),
                          tm=512, tn=1024, name="fox_in_proj")
    qa, ka, vt, edges = _fox_prep(h_main, hf, _pad_row(b_f, LANES),
                                  jnp.tile(q_norm_g, heads).reshape(1, heads * dh),
                                  jnp.tile(k_norm_g, heads).reshape(1, heads * dh),
                                  batch=batch, seq=seq, heads=heads, dh=dh, rows=FOX_TILE)
    return _fox_attn(qa, ka, vt, h_main, edges, q_norm_g, k_norm_g, batch=batch, seq=seq, heads=heads, dh=dh,
                     tile=FOX_TILE, unroll=8)


def kernel(x, p, ln_g, ln_b, ple_w_gate, ple_w_proj, gdn_w_in, gdn_conv_w, gdn_a_log, gdn_dt_bias,
           gdn_norm_g, gdn_w_out, fox_w_in, fox_b_f, fox_q_norm_g, fox_k_norm_g, fox_w_out):
    batch, seq, d = x.shape
    depth = ln_g.shape[0]
    alpha = (2 * depth) ** 0.25
    xf = x.reshape(batch * seq, d)
    p_rows = p.reshape(depth, batch * seq, -1)
    ln_g3, ln_b3 = ln_g.reshape(depth, 1, d), ln_b.reshape(depth, 1, d)
    w_gate, w_proj = ple_w_gate.astype(BF16), ple_w_proj.astype(BF16)
    gdn_w, fox_w = gdn_w_in.astype(BF16), fox_w_in.astype(BF16)
    gdn_wo, fox_wo = gdn_w_out.astype(BF16), fox_w_out.astype(BF16)
    for i in range(depth):
        j = i // 2
        if i % 2 == 0:
            a = _gdn_layer(xf, gdn_w, j, gdn_w_in[j], gdn_conv_w[j], gdn_a_log[j], gdn_dt_bias[j], gdn_norm_g[j],
                           batch=batch, seq=seq)
            w_out = gdn_wo
        else:
            a = _fox_layer(xf, fox_w, j, fox_w_in[j], fox_b_f[j], fox_q_norm_g[j], fox_k_norm_g[j],
                           batch=batch, seq=seq)
            w_out = fox_wo
        xf = _post(a, xf, p_rows, w_out, ln_g3, ln_b3, w_gate, w_proj, layer=i, mixer=j, alpha=alpha, tm=512)
    return xf.reshape(batch, seq, d)
```

```python
import functools

import jax
import jax.numpy as jnp
import numpy as np
from jax import lax
from jax.experimental import pallas as pl
from jax.experimental.pallas import tpu as pltpu

F32 = jnp.float32
BF16 = jnp.bfloat16

LANES = 128
V7X_VMEM_BYTES = 64 * 1024 * 1024
VMEM_LIMIT = V7X_VMEM_BYTES * 3 // 4

GDN_CHUNK = 64
LN_EPS = 1e-5
RMS_EPS = 1e-6
LOG2E = 1.4426950408889634

_NT = (((1,), (1,)), ((), ()))
_TN = (((0,), (0,)), ((), ()))


def _mm(a, b):
    return jnp.dot(a.astype(BF16), b.astype(BF16), preferred_element_type=F32)


def _sigmoid(x):
    return 1.0 / (1.0 + jnp.exp(-x))


def _silu(x):
    h = 0.5 * x
    return h + h * jnp.tanh(h)


def _softplus(x):
    return jnp.maximum(x, 0.0) + jnp.log1p(jnp.exp(-jnp.abs(x)))


def _split3(x):
    hi = x.astype(BF16).astype(F32)
    r = x - hi
    mid = r.astype(BF16).astype(F32)
    lo = (r - mid).astype(BF16).astype(F32)
    return hi, mid, lo


def _iota2(shape, dim):
    return lax.broadcasted_iota(jnp.int32, shape, dim)


def _params(*semantics):
    return pltpu.CompilerParams(dimension_semantics=semantics, vmem_limit_bytes=VMEM_LIMIT)


def _in_proj_kernel(x_ref, wm_ref, ws_ref, hm_ref, hs_ref, *, tn):
    xb = x_ref[...].astype(BF16)
    for j in range(wm_ref.shape[1] // tn):
        cols = slice(j * tn, (j + 1) * tn)
        hm_ref[:, cols] = jnp.dot(xb, wm_ref[:, cols], preferred_element_type=F32).astype(hm_ref.dtype)
    hs_ref[...] = jnp.dot(xb, ws_ref[...], preferred_element_type=F32)


def _in_proj(x, w_stack, layer, n, w_small, *, tm, tn, name):
    m, k = x.shape
    ns = w_small.shape[1]
    return pl.pallas_call(
        functools.partial(_in_proj_kernel, tn=tn),
        grid=(m // tm,),
        in_specs=[pl.BlockSpec((tm, k), lambda i: (i, 0)),
                  pl.BlockSpec((None, k, n), lambda i: (layer, 0, 0)),
                  pl.BlockSpec((k, ns), lambda i: (0, 0))],
        out_specs=[pl.BlockSpec((tm, n), lambda i: (i, 0)),
                   pl.BlockSpec((tm, ns), lambda i: (i, 0))],
        out_shape=[jax.ShapeDtypeStruct((m, n), BF16), jax.ShapeDtypeStruct((m, ns), F32)],
        compiler_params=_params("parallel"),
        name=name,
    )(x, w_stack, w_small)


def _gdn_gate_kernel(hs_ref, alog_ref, dtb_ref, beta_ref, gcum_ref, gcum_t_ref, *, rows):
    c = GDN_CHUNK
    beta_ref[...] = _sigmoid(hs_ref[:, :LANES])
    g = -jnp.exp(alog_ref[...]) * _softplus(hs_ref[:, LANES:] + dtb_ref[...])
    g3 = jnp.concatenate([part.astype(BF16) for part in _split3(g)], axis=1)
    tril = (_iota2((c, c), 0) >= _iota2((c, c), 1)).astype(BF16)
    eye = (_iota2((8, LANES), 0) == _iota2((8, LANES), 1)).astype(BF16)
    for i in range(rows // c):
        sums = jnp.dot(tril, g3[i * c:(i + 1) * c], preferred_element_type=F32)
        gc = sums[:, :LANES] + sums[:, LANES:2 * LANES] + sums[:, 2 * LANES:]
        gcum_ref[i * c:(i + 1) * c, :] = gc
        hi, mid, lo = (lax.dot_general(eye, part.astype(BF16), _NT, preferred_element_type=F32)
                       for part in _split3(gc))
        gcum_t_ref[i] = hi + mid + lo


def _gdn_gates(hs, alog_row, dtb_row, *, rows):
    m = hs.shape[0]
    c = GDN_CHUNK
    return pl.pallas_call(
        functools.partial(_gdn_gate_kernel, rows=rows),
        grid=(m // rows,),
        in_specs=[pl.BlockSpec((rows, 2 * LANES), lambda i: (i, 0)),
                  pl.BlockSpec((1, LANES), lambda i: (0, 0)),
                  pl.BlockSpec((1, LANES), lambda i: (0, 0))],
        out_specs=[pl.BlockSpec((rows, LANES), lambda i: (i, 0)),
                   pl.BlockSpec((rows, LANES), lambda i: (i, 0)),
                   pl.BlockSpec((rows // c, 8, c), lambda i: (i, 0, 0))],
        out_shape=[jax.ShapeDtypeStruct((m, LANES), F32),
                   jax.ShapeDtypeStruct((m, LANES), F32),
                   jax.ShapeDtypeStruct((m // c, 8, c), F32)],
        compiler_params=_params("parallel"),
        name="gdn_gates",
    )(hs, alog_row, dtb_row)


def _gdn_core_kernel(hq_ref, hk_ref, hv_ref, hz_ref, wq_ref, wk_ref, wv_ref, beta_ref, gcum_ref,
                     gcum_t_ref, ng_ref, o_ref, state_ref, xe_ref, *, rows, dk, hb):
    c = GDN_CHUNK
    n = rows // c
    group = pl.program_id(1)

    @pl.when(pl.program_id(2) == 0)
    def _():
        state_ref[...] = jnp.zeros_like(state_ref)
        xe_ref[:, :8, :] = jnp.zeros((3, 8, xe_ref.shape[2]), F32)

    def conv_silu(x_ref, w_ref, slot):
        w = w_ref[...]
        taps = w.shape[0]
        xe_ref[slot, 8:, :] = x_ref[...].astype(F32)
        y = w[taps - 1:taps, :] * xe_ref[slot, 8:, :]
        for t in range(taps - 1):
            y = y + w[t:t + 1, :] * xe_ref[slot, pl.ds(8 - taps + 1 + t, rows), :]
        xe_ref[slot, :8, :] = xe_ref[slot, rows:, :]
        return _silu(y)

    def l2norm(x):
        return x * lax.rsqrt(jnp.sum(x * x, axis=-1, keepdims=True) + RMS_EPS)

    k_all = conv_silu(hk_ref, wk_ref, 1)
    q_all = conv_silu(hq_ref, wq_ref, 0)
    v_all = conv_silu(hv_ref, wv_ref, 2)
    lane = _iota2((rows, LANES), 1)
    ri = _iota2((c, c), 0)
    ci = _iota2((c, c), 1)
    causal = ri >= ci
    strict = ri > ci
    eye = (ri == ci).astype(F32)
    dv = dk

    units = []
    for a in range(hb):
        head = group * hb + a
        cols = slice(a * dk, (a + 1) * dk)
        k = l2norm(k_all[:, cols])
        q = l2norm(q_all[:, cols]) * (dk ** -0.5)
        v = v_all[:, cols]
        beta = jnp.sum(jnp.where(lane == head, beta_ref[...], 0.0), axis=1, keepdims=True)
        gcol = jnp.sum(jnp.where(lane == head, gcum_ref[...], 0.0), axis=1, keepdims=True)
        for i in range(n):
            sl = slice(i * c, (i + 1) * c)
            units.append(dict(a=a, q=q[sl], k=k[sl], v=v[sl], beta=beta[sl], g=gcol[sl],
                              grow=gcum_t_ref[i, pl.ds(head, 1), :]))
    for u in units:
        u["glast"] = u["g"][c - 1:c, :]
        u["eg"] = jnp.exp(u["g"])
        u["decay"] = jnp.where(causal, jnp.exp(u["g"] - u["grow"]), 0.0)
        u["kb"] = u["k"] * u["beta"]
    for u in units:
        kbf = u["k"].astype(BF16)
        u["kk"] = lax.dot_general(u["kb"].astype(BF16), kbf, _NT, preferred_element_type=F32)
        u["qk"] = lax.dot_general(u["q"].astype(BF16), kbf, _NT, preferred_element_type=F32)
    for u in units:
        u["lmat"] = jnp.where(strict, u["kk"] * u["decay"], 0.0)
        u["amat"] = jnp.where(causal, u["qk"] * u["decay"], 0.0)
        u["tinv"] = eye - u["lmat"]
    for u in units:
        u["pk"] = _mm(u["lmat"], u["lmat"])
    span = 4
    while span < c:
        for u in units:
            u["both"] = _mm(u["pk"], jnp.concatenate([u["tinv"], u["pk"]], axis=1))
        for u in units:
            u["tinv"] = u["tinv"] + u["both"][:, :c]
            u["pk"] = u["both"][:, c:]
        span *= 2
    for u in units:
        u["tinv"] = u["tinv"] + _mm(u["pk"], u["tinv"])
    for u in units:
        u["x"] = _mm(u["tinv"], jnp.concatenate([u["v"] * u["beta"], u["kb"] * u["eg"]], axis=1))
    for u in units:
        u["ax"] = _mm(u["amat"], u["x"])
        kd = u["k"] * jnp.exp(u["glast"] - u["g"])
        u["kx"] = lax.dot_general(kd.astype(BF16), u["x"].astype(BF16), _TN, preferred_element_type=F32)
    for u in units:
        u["qeff"] = u["q"] * u["eg"] - u["ax"][:, dv:]
    states = [state_ref[a] for a in range(hb)]
    outs = [[] for _ in range(hb)]
    for i in range(n):
        for a in range(hb):
            u = units[a * n + i]
            sb = states[a].astype(BF16)
            outs[a].append(_mm(u["qeff"], sb) + u["ax"][:, :dv])
            states[a] = jnp.exp(u["glast"]) * states[a] + u["kx"][:, :dv] - _mm(u["kx"][:, dv:], sb)
    for a in range(hb):
        state_ref[a] = states[a]
        o = jnp.concatenate(outs[a], axis=0)
        o = o * lax.rsqrt(jnp.mean(o * o, axis=-1, keepdims=True) + RMS_EPS) * ng_ref[...]
        cols = slice(a * dk, (a + 1) * dk)
        o_ref[:, cols] = (o * _silu(hz_ref[:, cols].astype(F32))).astype(o_ref.dtype)


def _gdn_core(h_main, conv_w, beta, gcum, gcum_t, ng_row, *, batch, seq, heads, dk, rows, hb):
    m = h_main.shape[0]
    nb = seq // rows
    c = GDN_CHUNK
    taps = conv_w.shape[0]
    groups = heads // hb
    wide = hb * dk

    def col(off):
        return pl.BlockSpec((rows, wide), lambda b, h, j: (b * nb + j, off + h))

    def wcol(off):
        return pl.BlockSpec((taps, wide), lambda b, h, j: (0, off + h))

    def gate():
        return pl.BlockSpec((rows, LANES), lambda b, h, j: (b * nb + j, 0))

    return pl.pallas_call(
        functools.partial(_gdn_core_kernel, rows=rows, dk=dk, hb=hb),
        grid=(batch, groups, nb),
        in_specs=[col(0), col(groups), col(2 * groups), col(3 * groups),
                  wcol(0), wcol(groups), wcol(2 * groups),
                  gate(), gate(),
                  pl.BlockSpec((rows // c, 8, c), lambda b, h, j: (b * nb + j, 0, 0)),
                  pl.BlockSpec((1, dk), lambda b, h, j: (0, 0))],
        out_specs=pl.BlockSpec((rows, wide), lambda b, h, j: (b * nb + j, h)),
        out_shape=jax.ShapeDtypeStruct((m, heads * dk), BF16),
        scratch_shapes=[pltpu.VMEM((hb, dk, dk), F32), pltpu.VMEM((3, rows + 8, wide), F32)],
        compiler_params=_params("parallel", "parallel", "arbitrary"),
        name="gdn_core",
    )(h_main, h_main, h_main, h_main, conv_w, conv_w, conv_w, beta, gcum, gcum_t, ng_row)


FOX_TILE = 512
FOX_VROWS = 80


def _fox_prep_kernel(hq_ref, hk_ref, hv_ref, hf_ref, bf_ref, qg_ref, kg_ref, selq_ref, selk_ref,
                     qa_ref, ka_ref, vt_ref, edge_ref, carry_ref, *, rows, heads, dh):
    @pl.when(pl.program_id(1) == 0)
    def _():
        carry_ref[...] = jnp.zeros_like(carry_ref)

    logf = -_softplus(-(hf_ref[...] + bf_ref[...]))
    lane = _iota2((rows, LANES), 1)
    logf = jnp.where(lane < heads, logf, 0.0)
    tril = (_iota2((rows, rows), 0) >= _iota2((rows, rows), 1)).astype(BF16)
    sums = jnp.dot(tril, jnp.concatenate([part.astype(BF16) for part in _split3(logf)], axis=1),
                   preferred_element_type=F32)
    c = sums[:, :LANES] + sums[:, LANES:2 * LANES] + sums[:, 2 * LANES:] + carry_ref[...]
    carry_ref[...] = c[rows - 1:rows, :]
    c2 = c * LOG2E
    edge_ref[0] = jnp.concatenate([c2[:1], c2[rows - 1:], jnp.zeros((6, LANES), F32)], axis=0)
    chi, cmid, clo = _split3(c2)
    parts = (chi + pltpu.roll(cmid, heads, axis=1) + pltpu.roll(clo, 2 * heads, axis=1)
             + jnp.where(lane == 3 * heads, 1.0, 0.0)).astype(BF16)
    q_tails = jnp.dot(parts, selq_ref[...], preferred_element_type=F32)
    k_tails = jnp.dot(parts, selk_ref[...], preferred_element_type=F32)

    same_head = _iota2((LANES, LANES), 0) // dh == _iota2((LANES, LANES), 1) // dh
    head_mean = jnp.where(same_head, 1.0 / dh, 0.0).astype(BF16)

    def head_rms(x, g, scale):
        mean_sq = jnp.dot((x * x).astype(BF16), head_mean, preferred_element_type=F32)
        return x * lax.rsqrt(mean_sq + RMS_EPS) * (g * scale)

    per_block = LANES // dh
    vrow = _iota2((FOX_VROWS, LANES), 0)
    vlane = _iota2((FOX_VROWS, LANES), 1)
    out_row = _iota2((FOX_VROWS, rows), 0)
    for j in range(heads // per_block):
        cols = slice(j * LANES, (j + 1) * LANES)
        qn = head_rms(hq_ref[:, cols].astype(F32), qg_ref[:, cols], dh ** -0.5 * LOG2E)
        kn = head_rms(hk_ref[:, cols].astype(F32), kg_ref[:, cols], 1.0)
        vb = hv_ref[:, cols].astype(BF16)
        for a in range(per_block):
            h = j * per_block + a
            qh = qn if a == 0 else pltpu.roll(qn, LANES - a * dh, axis=1)
            kh = kn if a == 0 else pltpu.roll(kn, LANES - a * dh, axis=1)
            out_cols = slice(h * LANES, (h + 1) * LANES)
            qa_ref[:, out_cols] = jnp.where(lane < dh, qh, q_tails[:, out_cols]).astype(qa_ref.dtype)
            ka_ref[:, out_cols] = jnp.where(lane < dh, kh, k_tails[:, out_cols]).astype(ka_ref.dtype)
            pick = ((vlane == a * dh + vrow) & (vrow < dh)).astype(BF16)
            vt = lax.dot_general(pick, vb, _NT, preferred_element_type=F32)
            vt_ref[0, h] = jnp.where(out_row == dh, 1.0, vt).astype(vt_ref.dtype)


def _fox_bias_selectors(heads, dh):
    j = np.arange(LANES)[:, None]
    col = np.arange(heads * LANES)[None, :]
    h, l = col // LANES, col % LANES - dh
    part = lambda i: (l == i) & (j == i * heads + h)
    ones = lambda lo: (l >= lo) & (l < lo + 3) & (j == 3 * heads)
    sel_q = part(0) | part(1) | part(2) | ones(3)
    sel_k = ones(0).astype(np.float32) - ((l == 3) & (j == h)) - ((l == 4) & (j == heads + h)) \
        - ((l == 5) & (j == 2 * heads + h))
    return jnp.asarray(sel_q, BF16), jnp.asarray(sel_k, BF16)


def _fox_prep(h_main, hf, bf_row, qg_row, kg_row, *, batch, seq, heads, dh, rows):
    m = h_main.shape[0]
    w = heads * dh
    nt = seq // rows
    sel_q, sel_k = _fox_bias_selectors(heads, dh)

    def col(j):
        return pl.BlockSpec((rows, w), lambda b, t: (b * nt + t, j))

    def const(r, n):
        return pl.BlockSpec((r, n), lambda b, t: (0, 0))

    aug = pl.BlockSpec((rows, heads * LANES), lambda b, t: (b * nt + t, 0))
    return pl.pallas_call(
        functools.partial(_fox_prep_kernel, rows=rows, heads=heads, dh=dh),
        grid=(batch, nt),
        in_specs=[col(0), col(1), col(2),
                  pl.BlockSpec((rows, LANES), lambda b, t: (b * nt + t, 0)),
                  const(1, LANES), const(1, w), const(1, w),
                  const(LANES, heads * LANES), const(LANES, heads * LANES)],
        out_specs=[aug, aug,
                   pl.BlockSpec((1, heads, FOX_VROWS, rows), lambda b, t: (b, 0, 0, t)),
                   pl.BlockSpec((1, 8, LANES), lambda b, t: (b * nt + t, 0, 0))],
        out_shape=[jax.ShapeDtypeStruct((m, heads * LANES), BF16),
                   jax.ShapeDtypeStruct((m, heads * LANES), BF16),
                   jax.ShapeDtypeStruct((batch, heads, FOX_VROWS, seq), BF16),
                   jax.ShapeDtypeStruct((batch * nt, 8, LANES), F32)],
        scratch_shapes=[pltpu.VMEM((1, LANES), F32)],
        compiler_params=_params("parallel", "arbitrary"),
        name="fox_prep",
    )(h_main, h_main, h_main, hf, bf_row, qg_row, kg_row, sel_q, sel_k)


def _fox_attn_kernel(tq_ref, tk_ref, count_ref, q_ref, k_ref, v_ref, z_ref, o_ref, s0_ref, s1_ref, m_ref, acc_ref,
                     *, tile, dh, per_block, nq, unroll, heads, tasks):
    half = tile // 2
    below_half = _iota2((half, half), 0) <= _iota2((half, half), 1)

    def logits(a, qi, kj, s_ref):
        qs = pl.multiple_of(jnp.minimum(qi, nq - 1) * tile, tile)
        ks = pl.multiple_of(kj * tile, tile)
        s_ref[a] = lax.dot_general(k_ref[pl.ds(ks, tile), a * LANES:(a + 1) * LANES],
                                   q_ref[pl.ds(qs, tile), a * LANES:(a + 1) * LANES], _NT,
                                   preferred_element_type=F32)

    def diagonal_logits(a, d, s_ref):
        ds = pl.multiple_of(d * tile, tile)
        lanes = slice(a * LANES, (a + 1) * LANES)
        s_ref[a, :half, :] = lax.dot_general(k_ref[pl.ds(ds, half), lanes], q_ref[pl.ds(ds, tile), lanes], _NT,
                                             preferred_element_type=F32)
        s_ref[a, half:, half:] = lax.dot_general(k_ref[pl.ds(ds + half, half), lanes],
                                                 q_ref[pl.ds(ds + half, half), lanes], _NT,
                                                 preferred_element_type=F32)

    def diagonal_softmax_pv(a, d, s_ref):
        ds = pl.multiple_of(d * tile, tile)
        early = jnp.where(below_half, s_ref[a, :half, :half], -jnp.inf)
        cross = s_ref[a, :half, half:]
        late = jnp.where(below_half, s_ref[a, half:, half:], -jnp.inf)
        m_early = jnp.max(early, axis=0, keepdims=True)
        m_late = jnp.maximum(jnp.max(cross, axis=0, keepdims=True), jnp.max(late, axis=0, keepdims=True))
        p_top = jnp.concatenate([jnp.exp2(early - m_early), jnp.exp2(cross - m_late)], axis=1).astype(BF16)
        p_late = jnp.exp2(late - m_late).astype(BF16)
        top = jnp.dot(v_ref[0, a, :, pl.ds(ds, half)], p_top, preferred_element_type=F32)
        bottom = jnp.dot(v_ref[0, a, :, pl.ds(ds + half, half)], p_late, preferred_element_type=F32)
        acc_ref[d, a, :, :half] = top[:, :half]
        acc_ref[d, a, :, half:] = top[:, half:] + bottom
        m_ref[d, a] = jnp.concatenate([m_early, m_late], axis=1)

    def softmax_pv(a, qi, kj, s_ref):
        ks = pl.multiple_of(kj * tile, tile)
        vt = v_ref[0, a, :, pl.ds(ks, tile)]
        s = s_ref[a]
        m_prev = m_ref[qi, a]
        m_new = jnp.maximum(m_prev, jnp.max(s, axis=0, keepdims=True))
        p = jnp.exp2(s - m_new).astype(BF16)
        acc_ref[qi, a] = jnp.exp2(m_prev - m_new) * acc_ref[qi, a] + jnp.dot(vt, p, preferred_element_type=F32)
        m_ref[qi, a] = m_new

    bufs = (s0_ref, s1_ref)
    everyone = range(per_block)
    m_ref[nq] = jnp.zeros(m_ref.shape[1:], F32)
    acc_ref[nq] = jnp.zeros(acc_ref.shape[1:], F32)

    for a in everyone:
        diagonal_logits(a, 0, s0_ref)

    def diagonal_steps(u, _):
        for r in range(unroll):
            d = unroll * u + r
            nxt = jnp.minimum(d + 1, nq - 1)
            for a in everyone:
                diagonal_logits(a, nxt, bufs[(r + 1) % 2])
            for a in everyone:
                diagonal_softmax_pv(a, d, bufs[r % 2])
        return 0

    lax.fori_loop(0, nq // unroll, diagonal_steps, 0)

    for a in everyone:
        head = pl.program_id(1) * per_block + a
        base = (pl.program_id(0) * heads + head) * tasks
        logits(a, tq_ref[base], tk_ref[base], s0_ref)

        def lower_steps(count, first, a=a, base=base):
            def run(u, _):
                for r in range(count):
                    t = base + first + count * u + r
                    nxt = jnp.minimum(t + 1, base + tasks - 1)
                    logits(a, tq_ref[nxt], tk_ref[nxt], bufs[(r + 1) % 2])
                    softmax_pv(a, tq_ref[t], tk_ref[t], bufs[r % 2])
                return 0
            return run

        kept = count_ref[pl.program_id(0) * heads + head]
        main = kept // unroll
        lax.fori_loop(0, main, lower_steps(unroll, 0), 0)
        lax.fori_loop(0, (kept - main * unroll + 1) // 2, lower_steps(2, main * unroll), 0)

    def finish(qi, _):
        qs = pl.multiple_of(qi * tile, tile)
        o_t = jnp.concatenate([acc_ref[qi, a, :dh, :] / acc_ref[qi, a, dh:dh + 1, :] for a in everyone],
                              axis=0)
        o_ref[pl.ds(qs, tile), :] = (o_t.T * _silu(z_ref[pl.ds(qs, tile), :].astype(F32))).astype(o_ref.dtype)
        return 0

    lax.fori_loop(0, nq, finish, 0)


FOX_UNDERFLOW_LOG2 = 150.0
FOX_NORM_SLACK = 1.02
FOX_BIAS_SLACK = 2.0


def _fox_task_lists(edges, q_gain, k_gain, *, batch, heads, dh, nq):
    qk_bound = FOX_NORM_SLACK * dh ** 0.5 * LOG2E * jnp.max(jnp.abs(q_gain)) * jnp.max(jnp.abs(k_gain))
    threshold = 2.0 * qk_bound + FOX_UNDERFLOW_LOG2 + FOX_BIAS_SLACK
    edges = edges.reshape(batch, nq, 8, LANES)
    first = edges[:, :, 0, :heads].transpose(0, 2, 1)
    last = edges[:, :, 1, :heads].transpose(0, 2, 1)
    qi = jnp.arange(nq)[:, None]
    kj = jnp.arange(nq)[None, :]
    keep = (kj < qi) & (first[..., :, None] - last[..., None, :] >= -threshold)
    start = jnp.min(jnp.where(keep, kj, qi), axis=-1)
    length = jnp.arange(nq) - start
    ends = jnp.cumsum(length, axis=-1)
    count = ends[..., -1]
    tasks = nq * (nq - 1) // 2
    t = jnp.arange(tasks)
    row = jnp.sum(ends[..., None, :] <= t[:, None], axis=-1)
    in_row = row[..., None] == jnp.arange(nq)
    tk = t + jnp.sum(jnp.where(in_row, (start - ends + length)[..., None, :], 0), axis=-1)
    tq = jnp.where(t < count[..., None], row, nq).astype(jnp.int32)
    tk = jnp.where(t < count[..., None], tk, 0).astype(jnp.int32)
    return tq.reshape(-1), tk.reshape(-1), count.astype(jnp.int32).reshape(-1), tasks


def _fox_attn(qa, ka, vt, h_main, edges, q_gain, k_gain, *, batch, seq, heads, dh, tile, unroll):
    m = qa.shape[0]
    nq = seq // tile
    assert unroll % 2 == 0 and nq % unroll == 0 and (nq * (nq - 1) // 2) % 2 == 0
    per_block = LANES // dh
    blocks = heads // per_block
    z_off = 3 * blocks
    wide = per_block * LANES
    tq, tk, count, tasks = _fox_task_lists(edges, q_gain, k_gain, batch=batch, heads=heads, dh=dh, nq=nq)
    grid_spec = pltpu.PrefetchScalarGridSpec(
        num_scalar_prefetch=3,
        grid=(batch, blocks),
        in_specs=[pl.BlockSpec((seq, wide), lambda b, h, *_: (b, h)),
                  pl.BlockSpec((seq, wide), lambda b, h, *_: (b, h)),
                  pl.BlockSpec((1, per_block, FOX_VROWS, seq), lambda b, h, *_: (b, h, 0, 0)),
                  pl.BlockSpec((seq, LANES), lambda b, h, *_: (b, z_off + h))],
        out_specs=pl.BlockSpec((seq, LANES), lambda b, h, *_: (b, h)),
        scratch_shapes=[pltpu.VMEM((per_block, tile, tile), F32), pltpu.VMEM((per_block, tile, tile), F32),
                        pltpu.VMEM((nq + 1, per_block, 1, tile), F32),
                        pltpu.VMEM((nq + 1, per_block, FOX_VROWS, tile), F32)])
    return pl.pallas_call(
        functools.partial(_fox_attn_kernel, tile=tile, dh=dh, per_block=per_block, nq=nq, unroll=unroll,
                          heads=heads, tasks=tasks),
        grid_spec=grid_spec,
        out_shape=jax.ShapeDtypeStruct((m, heads * dh), BF16),
        compiler_params=_params("parallel", "parallel"),
        name="fox_attn",
    )(tq, tk, count, qa, ka, vt, h_main)


def _post_kernel(a_ref, x_ref, p_ref, wo_ref, g_ref, b_ref, wg_ref, wp_ref, o_ref, *, alpha, parts):
    sub = a_ref.shape[0] // parts
    blocks = [slice(r * sub, (r + 1) * sub) for r in range(parts)]
    proj = [jnp.dot(a_ref[rs, :], wo_ref[...], preferred_element_type=F32) for rs in blocks]
    emb = [_mm(p_ref[rs, :], wp_ref[...]) for rs in blocks]
    normed = []
    for rs, y in zip(blocks, proj):
        t = alpha * x_ref[rs, :] + y
        mu = jnp.mean(t, axis=-1, keepdims=True)
        d = t - mu
        var = jnp.mean(d * d, axis=-1, keepdims=True)
        normed.append(d * lax.rsqrt(var + LN_EPS) * g_ref[...] + b_ref[...])
    gates = [_mm(xn, wg_ref[...]) for xn in normed]
    for rs, xn, gate, e in zip(blocks, normed, gates, emb):
        o_ref[rs, :] = xn + _sigmoid(gate) * e


def _post(a, x, p, w_out, ln_g, ln_b, w_gate, w_proj, *, layer, mixer, alpha, tm):
    m, d = x.shape
    dp = p.shape[2]

    def rows(n):
        return pl.BlockSpec((tm, n), lambda i: (i, 0))

    def of_layer(index, r, n):
        return pl.BlockSpec((None, r, n), lambda i: (index, 0, 0))

    return pl.pallas_call(
        functools.partial(_post_kernel, alpha=alpha, parts=2),
        grid=(m // tm,),
        in_specs=[rows(a.shape[1]), rows(d), pl.BlockSpec((None, tm, dp), lambda i: (layer, i, 0)),
                  of_layer(mixer, a.shape[1], d), of_layer(layer, 1, d), of_layer(layer, 1, d),
                  of_layer(layer, d, d), of_layer(layer, dp, d)],
        out_specs=rows(d),
        out_shape=jax.ShapeDtypeStruct((m, d), F32),
        compiler_params=_params("parallel"),
        name="post_block",
    )(a, x, p, w_out, ln_g, ln_b, w_gate, w_proj)


def _pad_cols(w, n):
    return jnp.pad(w, ((0, 0), (0, n - w.shape[1])))


def _pad_row(v, n):
    return jnp.pad(v, (0, n - v.shape[0])).reshape(1, n)


def _gdn_layer(x, w_stack, layer, w_in, conv_w, a_log, dt_bias, norm_g, *, batch, seq):
    heads = a_log.shape[0]
    dk = norm_g.shape[0]
    wide = 4 * heads * dk
    w_gates = jnp.concatenate([_pad_cols(w_in[:, wide:wide + heads], LANES),
                               _pad_cols(w_in[:, wide + heads:], LANES)], axis=1).astype(BF16)
    h_main, hs = _in_proj(x, w_stack, layer, wide, w_gates, tm=512, tn=1024, name="gdn_in_proj")
    beta, gcum, gcum_t = _gdn_gates(hs, _pad_row(a_log, LANES), _pad_row(dt_bias, LANES), rows=512)
    return _gdn_core(h_main, conv_w, beta, gcum, gcum_t, norm_g.reshape(1, dk),
                     batch=batch, seq=seq, heads=heads, dk=dk, rows=512, hb=8)


def _fox_layer(x, w_stack, layer, w_in, b_f, q_norm_g, k_norm_g, *, batch, seq):
    heads = b_f.shape[0]
    dh = q_norm_g.shape[0]
    wide = 4 * heads * dh
    h_main, hf = _in_proj(x, w_stack, layer, wide, _pad_cols(w_in[:, wide:], LANES).astype(BF16),
                          tm=512, tn=1024, name="fox_in_proj")
    qa, ka, vt, edges = _fox_prep(h_main, hf, _pad_row(b_f, LANES),
                                  jnp.tile(q_norm_g, heads).reshape(1, heads * dh),
                                  jnp.tile(k_norm_g, heads).reshape(1, heads * dh),
                                  batch=batch, seq=seq, heads=heads, dh=dh, rows=FOX_TILE)
    return _fox_attn(qa, ka, vt, h_main, edges, q_norm_g, k_norm_g, batch=batch, seq=seq, heads=heads, dh=dh,
                     tile=FOX_TILE, unroll=8)


def kernel(x, p, ln_g, ln_b, ple_w_gate, ple_w_proj, gdn_w_in, gdn_conv_w, gdn_a_log, gdn_dt_bias,
           gdn_norm_g, gdn_w_out, fox_w_in, fox_b_f, fox_q_norm_g, fox_k_norm_g, fox_w_out):
    batch, seq, d = x.shape
    depth = ln_g.shape[0]
    alpha = (2 * depth) ** 0.25
    xf = x.reshape(batch * seq, d)
    p_rows = p.reshape(depth, batch * seq, -1)
    ln_g3, ln_b3 = ln_g.reshape(depth, 1, d), ln_b.reshape(depth, 1, d)
    w_gate, w_proj = ple_w_gate.astype(BF16), ple_w_proj.astype(BF16)
    gdn_w, fox_w = gdn_w_in.astype(BF16), fox_w_in.astype(BF16)
    gdn_wo, fox_wo = gdn_w_out.astype(BF16), fox_w_out.astype(BF16)
    for i in range(depth):
        j = i // 2
        if i % 2 == 0:
            a = _gdn_layer(xf, gdn_w, j, gdn_w_in[j], gdn_conv_w[j], gdn_a_log[j], gdn_dt_bias[j], gdn_norm_g[j],
                           batch=batch, seq=seq)
            w_out = gdn_wo
        else:
            a = _fox_layer(xf, fox_w, j, fox_w_in[j], fox_b_f[j], fox_q_norm_g[j], fox_k_norm_g[j],
                           batch=batch, seq=seq)
            w_out = fox_wo
        xf = _post(a, xf, p_rows, w_out, ln_g3, ln_b3, w_gate, w_proj, layer=i, mixer=j, alpha=alpha, tm=512)
    return xf.reshape(batch, seq, d)
```

```python
import functools

import jax
import jax.numpy as jnp
import numpy as np
from jax import lax
from jax.experimental import pallas as pl
from jax.experimental.pallas import tpu as pltpu

F32 = jnp.float32
BF16 = jnp.bfloat16

LANES = 128
V7X_VMEM_BYTES = 64 * 1024 * 1024
VMEM_LIMIT = V7X_VMEM_BYTES * 3 // 4

GDN_CHUNK = 64
LN_EPS = 1e-5
RMS_EPS = 1e-6
LOG2E = 1.4426950408889634

_NT = (((1,), (1,)), ((), ()))
_TN = (((0,), (0,)), ((), ()))


def _mm(a, b):
    return jnp.dot(a.astype(BF16), b.astype(BF16), preferred_element_type=F32)


def _sigmoid(x):
    return 1.0 / (1.0 + jnp.exp(-x))


def _silu(x):
    h = 0.5 * x
    return h + h * jnp.tanh(h)


def _softplus(x):
    return jnp.maximum(x, 0.0) + jnp.log1p(jnp.exp(-jnp.abs(x)))


def _split3(x):
    hi = x.astype(BF16).astype(F32)
    r = x - hi
    mid = r.astype(BF16).astype(F32)
    lo = (r - mid).astype(BF16).astype(F32)
    return hi, mid, lo


def _iota2(shape, dim):
    return lax.broadcasted_iota(jnp.int32, shape, dim)


def _params(*semantics):
    return pltpu.CompilerParams(dimension_semantics=semantics, vmem_limit_bytes=VMEM_LIMIT)


def _in_proj_kernel(x_ref, wm_ref, ws_ref, hm_ref, hs_ref, *, tn):
    xb = x_ref[...].astype(BF16)
    for j in range(wm_ref.shape[1] // tn):
        cols = slice(j * tn, (j + 1) * tn)
        hm_ref[:, cols] = jnp.dot(xb, wm_ref[:, cols], preferred_element_type=F32).astype(hm_ref.dtype)
    hs_ref[...] = jnp.dot(xb, ws_ref[...], preferred_element_type=F32)


def _in_proj(x, w_stack, layer, n, w_small, *, tm, tn, name):
    m, k = x.shape
    ns = w_small.shape[1]
    return pl.pallas_call(
        functools.partial(_in_proj_kernel, tn=tn),
        grid=(m // tm,),
        in_specs=[pl.BlockSpec((tm, k), lambda i: (i, 0)),
                  pl.BlockSpec((None, k, n), lambda i: (layer, 0, 0)),
                  pl.BlockSpec((k, ns), lambda i: (0, 0))],
        out_specs=[pl.BlockSpec((tm, n), lambda i: (i, 0)),
                   pl.BlockSpec((tm, ns), lambda i: (i, 0))],
        out_shape=[jax.ShapeDtypeStruct((m, n), BF16), jax.ShapeDtypeStruct((m, ns), F32)],
        compiler_params=_params("parallel"),
        name=name,
    )(x, w_stack, w_small)


def _gdn_gate_kernel(hs_ref, alog_ref, dtb_ref, beta_ref, gcum_ref, gcum_t_ref, *, rows):
    c = GDN_CHUNK
    beta_ref[...] = _sigmoid(hs_ref[:, :LANES])
    g = -jnp.exp(alog_ref[...]) * _softplus(hs_ref[:, LANES:] + dtb_ref[...])
    g3 = jnp.concatenate([part.astype(BF16) for part in _split3(g)], axis=1)
    tril = (_iota2((c, c), 0) >= _iota2((c, c), 1)).astype(BF16)
    eye = (_iota2((8, LANES), 0) == _iota2((8, LANES), 1)).astype(BF16)
    for i in range(rows // c):
        sums = jnp.dot(tril, g3[i * c:(i + 1) * c], preferred_element_type=F32)
        gc = sums[:, :LANES] + sums[:, LANES:2 * LANES] + sums[:, 2 * LANES:]
        gcum_ref[i * c:(i + 1) * c, :] = gc
        hi, mid, lo = (lax.dot_general(eye, part.astype(BF16), _NT, preferred_element_type=F32)
                       for part in _split3(gc))
        gcum_t_ref[i] = hi + mid + lo


def _gdn_gates(hs, alog_row, dtb_row, *, rows):
    m = hs.shape[0]
    c = GDN_CHUNK
    return pl.pallas_call(
        functools.partial(_gdn_gate_kernel, rows=rows),
        grid=(m // rows,),
        in_specs=[pl.BlockSpec((rows, 2 * LANES), lambda i: (i, 0)),
                  pl.BlockSpec((1, LANES), lambda i: (0, 0)),
                  pl.BlockSpec((1, LANES), lambda i: (0, 0))],
        out_specs=[pl.BlockSpec((rows, LANES), lambda i: (i, 0)),
                   pl.BlockSpec((rows, LANES), lambda i: (i, 0)),
                   pl.BlockSpec((rows // c, 8, c), lambda i: (i, 0, 0))],
        out_shape=[jax.ShapeDtypeStruct((m, LANES), F32),
                   jax.ShapeDtypeStruct((m, LANES), F32),
                   jax.ShapeDtypeStruct((m // c, 8, c), F32)],
        compiler_params=_params("parallel"),
        name="gdn_gates",
    )(hs, alog_row, dtb_row)


def _gdn_core_kernel(hq_ref, hk_ref, hv_ref, hz_ref, wq_ref, wk_ref, wv_ref, beta_ref, gcum_ref,
                     gcum_t_ref, ng_ref, o_ref, state_ref, xe_ref, *, rows, dk, hb):
    c = GDN_CHUNK
    n = rows // c
    group = pl.program_id(1)

    @pl.when(pl.program_id(2) == 0)
    def _():
        state_ref[...] = jnp.zeros_like(state_ref)
        xe_ref[:, :8, :] = jnp.zeros((3, 8, xe_ref.shape[2]), F32)

    def conv_silu(x_ref, w_ref, slot):
        w = w_ref[...]
        taps = w.shape[0]
        xe_ref[slot, 8:, :] = x_ref[...].astype(F32)
        y = w[taps - 1:taps, :] * xe_ref[slot, 8:, :]
        for t in range(taps - 1):
            y = y + w[t:t + 1, :] * xe_ref[slot, pl.ds(8 - taps + 1 + t, rows), :]
        xe_ref[slot, :8, :] = xe_ref[slot, rows:, :]
        return _silu(y)

    def l2norm(x):
        return x * lax.rsqrt(jnp.sum(x * x, axis=-1, keepdims=True) + RMS_EPS)

    k_all = conv_silu(hk_ref, wk_ref, 1)
    q_all = conv_silu(hq_ref, wq_ref, 0)
    v_all = conv_silu(hv_ref, wv_ref, 2)
    lane = _iota2((rows, LANES), 1)
    ri = _iota2((c, c), 0)
    ci = _iota2((c, c), 1)
    causal = ri >= ci
    strict = ri > ci
    eye = (ri == ci).astype(F32)
    dv = dk

    units = []
    for a in range(hb):
        head = group * hb + a
        cols = slice(a * dk, (a + 1) * dk)
        k = l2norm(k_all[:, cols])
        q = l2norm(q_all[:, cols]) * (dk ** -0.5)
        v = v_all[:, cols]
        beta = jnp.sum(jnp.where(lane == head, beta_ref[...], 0.0), axis=1, keepdims=True)
        gcol = jnp.sum(jnp.where(lane == head, gcum_ref[...], 0.0), axis=1, keepdims=True)
        for i in range(n):
            sl = slice(i * c, (i + 1) * c)
            units.append(dict(a=a, q=q[sl], k=k[sl], v=v[sl], beta=beta[sl], g=gcol[sl],
                              grow=gcum_t_ref[i, pl.ds(head, 1), :]))
    for u in units:
        u["glast"] = u["g"][c - 1:c, :]
        u["eg"] = jnp.exp(u["g"])
        u["decay"] = jnp.where(causal, jnp.exp(u["g"] - u["grow"]), 0.0)
        u["kb"] = u["k"] * u["beta"]
    for u in units:
        kbf = u["k"].astype(BF16)
        u["kk"] = lax.dot_general(u["kb"].astype(BF16), kbf, _NT, preferred_element_type=F32)
        u["qk"] = lax.dot_general(u["q"].astype(BF16), kbf, _NT, preferred_element_type=F32)
    for u in units:
        u["lmat"] = jnp.where(strict, u["kk"] * u["decay"], 0.0)
        u["amat"] = jnp.where(causal, u["qk"] * u["decay"], 0.0)
        u["tinv"] = eye - u["lmat"]
    for u in units:
        u["pk"] = _mm(u["lmat"], u["lmat"])
    span = 4
    while span < c:
        for u in units:
            u["both"] = _mm(u["pk"], jnp.concatenate([u["tinv"], u["pk"]], axis=1))
        for u in units:
            u["tinv"] = u["tinv"] + u["both"][:, :c]
            u["pk"] = u["both"][:, c:]
        span *= 2
    for u in units:
        u["tinv"] = u["tinv"] + _mm(u["pk"], u["tinv"])
    for u in units:
        u["x"] = _mm(u["tinv"], jnp.concatenate([u["v"] * u["beta"], u["kb"] * u["eg"]], axis=1))
    for u in units:
        u["ax"] = _mm(u["amat"], u["x"])
        kd = u["k"] * jnp.exp(u["glast"] - u["g"])
        u["kx"] = lax.dot_general(kd.astype(BF16), u["x"].astype(BF16), _TN, preferred_element_type=F32)
    for u in units:
        u["qeff"] = u["q"] * u["eg"] - u["ax"][:, dv:]
    states = [state_ref[a] for a in range(hb)]
    outs = [[] for _ in range(hb)]
    for i in range(n):
        for a in range(hb):
            u = units[a * n + i]
            sb = states[a].astype(BF16)
            outs[a].append(_mm(u["qeff"], sb) + u["ax"][:, :dv])
            states[a] = jnp.exp(u["glast"]) * states[a] + u["kx"][:, :dv] - _mm(u["kx"][:, dv:], sb)
    for a in range(hb):
        state_ref[a] = states[a]
        o = jnp.concatenate(outs[a], axis=0)
        o = o * lax.rsqrt(jnp.mean(o * o, axis=-1, keepdims=True) + RMS_EPS) * ng_ref[...]
        cols = slice(a * dk, (a + 1) * dk)
        o_ref[:, cols] = (o * _silu(hz_ref[:, cols].astype(F32))).astype(o_ref.dtype)


def _gdn_core(h_main, conv_w, beta, gcum, gcum_t, ng_row, *, batch, seq, heads, dk, rows, hb):
    m = h_main.shape[0]
    nb = seq // rows
    c = GDN_CHUNK
    taps = conv_w.shape[0]
    groups = heads // hb
    wide = hb * dk

    def col(off):
        return pl.BlockSpec((rows, wide), lambda b, h, j: (b * nb + j, off + h))

    def wcol(off):
        return pl.BlockSpec((taps, wide), lambda b, h, j: (0, off + h))

    def gate():
        return pl.BlockSpec((rows, LANES), lambda b, h, j: (b * nb + j, 0))

    return pl.pallas_call(
        functools.partial(_gdn_core_kernel, rows=rows, dk=dk, hb=hb),
        grid=(batch, groups, nb),
        in_specs=[col(0), col(groups), col(2 * groups), col(3 * groups),
                  wcol(0), wcol(groups), wcol(2 * groups),
                  gate(), gate(),
                  pl.BlockSpec((rows // c, 8, c), lambda b, h, j: (b * nb + j, 0, 0)),
                  pl.BlockSpec((1, dk), lambda b, h, j: (0, 0))],
        out_specs=pl.BlockSpec((rows, wide), lambda b, h, j: (b * nb + j, h)),
        out_shape=jax.ShapeDtypeStruct((m, heads * dk), BF16),
        scratch_shapes=[pltpu.VMEM((hb, dk, dk), F32), pltpu.VMEM((3, rows + 8, wide), F32)],
        compiler_params=_params("parallel", "parallel", "arbitrary"),
        name="gdn_core",
    )(h_main, h_main, h_main, h_main, conv_w, conv_w, conv_w, beta, gcum, gcum_t, ng_row)


FOX_TILE = 512
FOX_VROWS = 80


def _fox_prep_kernel(hq_ref, hk_ref, hv_ref, hf_ref, bf_ref, qg_ref, kg_ref, selq_ref, selk_ref,
                     qa_ref, ka_ref, vt_ref, edge_ref, carry_ref, *, rows, heads, dh):
    @pl.when(pl.program_id(1) == 0)
    def _():
        carry_ref[...] = jnp.zeros_like(carry_ref)

    logf = -_softplus(-(hf_ref[...] + bf_ref[...]))
    lane = _iota2((rows, LANES), 1)
    logf = jnp.where(lane < heads, logf, 0.0)
    tril = (_iota2((rows, rows), 0) >= _iota2((rows, rows), 1)).astype(BF16)
    sums = jnp.dot(tril, jnp.concatenate([part.astype(BF16) for part in _split3(logf)], axis=1),
                   preferred_element_type=F32)
    c = sums[:, :LANES] + sums[:, LANES:2 * LANES] + sums[:, 2 * LANES:] + carry_ref[...]
    carry_ref[...] = c[rows - 1:rows, :]
    c2 = c * LOG2E
    edge_ref[0] = jnp.concatenate([c2[:1], c2[rows - 1:], jnp.zeros((6, LANES), F32)], axis=0)
    chi, cmid, clo = _split3(c2)
    parts = (chi + pltpu.roll(cmid, heads, axis=1) + pltpu.roll(clo, 2 * heads, axis=1)
             + jnp.where(lane == 3 * heads, 1.0, 0.0)).astype(BF16)
    q_tails = jnp.dot(parts, selq_ref[...], preferred_element_type=F32)
    k_tails = jnp.dot(parts, selk_ref[...], preferred_element_type=F32)

    same_head = _iota2((LANES, LANES), 0) // dh == _iota2((LANES, LANES), 1) // dh
    head_mean = jnp.where(same_head, 1.0 / dh, 0.0).astype(BF16)

    def head_rms(x, g, scale):
        mean_sq = jnp.dot((x * x).astype(BF16), head_mean, preferred_element_type=F32)
        return x * lax.rsqrt(mean_sq + RMS_EPS) * (g * scale)

    per_block = LANES // dh
    vrow = _iota2((FOX_VROWS, LANES), 0)
    vlane = _iota2((FOX_VROWS, LANES), 1)
    out_row = _iota2((FOX_VROWS, rows), 0)
    for j in range(heads // per_block):
        cols = slice(j * LANES, (j + 1) * LANES)
        qn = head_rms(hq_ref[:, cols].astype(F32), qg_ref[:, cols], dh ** -0.5 * LOG2E)
        kn = head_rms(hk_ref[:, cols].astype(F32), kg_ref[:, cols], 1.0)
        vb = hv_ref[:, cols].astype(BF16)
        for a in range(per_block):
            h = j * per_block + a
            qh = qn if a == 0 else pltpu.roll(qn, LANES - a * dh, axis=1)
            kh = kn if a == 0 else pltpu.roll(kn, LANES - a * dh, axis=1)
            out_cols = slice(h * LANES, (h + 1) * LANES)
            qa_ref[:, out_cols] = jnp.where(lane < dh, qh, q_tails[:, out_cols]).astype(qa_ref.dtype)
            ka_ref[:, out_cols] = jnp.where(lane < dh, kh, k_tails[:, out_cols]).astype(ka_ref.dtype)
            pick = ((vlane == a * dh + vrow) & (vrow < dh)).astype(BF16)
            vt = lax.dot_general(pick, vb, _NT, preferred_element_type=F32)
            vt_ref[0, h] = jnp.where(out_row == dh, 1.0, vt).astype(vt_ref.dtype)


def _fox_bias_selectors(heads, dh):
    j = np.arange(LANES)[:, None]
    col = np.arange(heads * LANES)[None, :]
    h, l = col // LANES, col % LANES - dh
    part = lambda i: (l == i) & (j == i * heads + h)
    ones = lambda lo: (l >= lo) & (l < lo + 3) & (j == 3 * heads)
    sel_q = part(0) | part(1) | part(2) | ones(3)
    sel_k = ones(0).astype(np.float32) - ((l == 3) & (j == h)) - ((l == 4) & (j == heads + h)) \
        - ((l == 5) & (j == 2 * heads + h))
    return jnp.asarray(sel_q, BF16), jnp.asarray(sel_k, BF16)


def _fox_prep(h_main, hf, bf_row, qg_row, kg_row, *, batch, seq, heads, dh, rows):
    m = h_main.shape[0]
    w = heads * dh
    nt = seq // rows
    sel_q, sel_k = _fox_bias_selectors(heads, dh)

    def col(j):
        return pl.BlockSpec((rows, w), lambda b, t: (b * nt + t, j))

    def const(r, n):
        return pl.BlockSpec((r, n), lambda b, t: (0, 0))

    aug = pl.BlockSpec((rows, heads * LANES), lambda b, t: (b * nt + t, 0))
    return pl.pallas_call(
        functools.partial(_fox_prep_kernel, rows=rows, heads=heads, dh=dh),
        grid=(batch, nt),
        in_specs=[col(0), col(1), col(2),
                  pl.BlockSpec((rows, LANES), lambda b, t: (b * nt + t, 0)),
                  const(1, LANES), const(1, w), const(1, w),
                  const(LANES, heads * LANES), const(LANES, heads * LANES)],
        out_specs=[aug, aug,
                   pl.BlockSpec((1, heads, FOX_VROWS, rows), lambda b, t: (b, 0, 0, t)),
                   pl.BlockSpec((1, 8, LANES), lambda b, t: (b * nt + t, 0, 0))],
        out_shape=[jax.ShapeDtypeStruct((m, heads * LANES), BF16),
                   jax.ShapeDtypeStruct((m, heads * LANES), BF16),
                   jax.ShapeDtypeStruct((batch, heads, FOX_VROWS, seq), BF16),
                   jax.ShapeDtypeStruct((batch * nt, 8, LANES), F32)],
        scratch_shapes=[pltpu.VMEM((1, LANES), F32)],
        compiler_params=_params("parallel", "arbitrary"),
        name="fox_prep",
    )(h_main, h_main, h_main, hf, bf_row, qg_row, kg_row, sel_q, sel_k)


def _fox_attn_kernel(tq_ref, tk_ref, count_ref, q_ref, k_ref, v_ref, z_ref, o_ref, s0_ref, s1_ref, m_ref, acc_ref,
                     *, tile, dh, per_block, nq, unroll, heads, tasks):
    half = tile // 2
    below_half = _iota2((half, half), 0) <= _iota2((half, half), 1)

    def logits(a, qi, kj, s_ref):
        qs = pl.multiple_of(jnp.minimum(qi, nq - 1) * tile, tile)
        ks = pl.multiple_of(kj * tile, tile)
        s_ref[a] = lax.dot_general(k_ref[pl.ds(ks, tile), a * LANES:(a + 1) * LANES],
                                   q_ref[pl.ds(qs, tile), a * LANES:(a + 1) * LANES], _NT,
                                   preferred_element_type=F32)

    def diagonal_logits(a, d, s_ref):
        ds = pl.multiple_of(d * tile, tile)
        lanes = slice(a * LANES, (a + 1) * LANES)
        s_ref[a, :half, :] = lax.dot_general(k_ref[pl.ds(ds, half), lanes], q_ref[pl.ds(ds, tile), lanes], _NT,
                                             preferred_element_type=F32)
        s_ref[a, half:, half:] = lax.dot_general(k_ref[pl.ds(ds + half, half), lanes],
                                                 q_ref[pl.ds(ds + half, half), lanes], _NT,
                                                 preferred_element_type=F32)

    def diagonal_softmax_pv(a, d, s_ref):
        ds = pl.multiple_of(d * tile, tile)
        early = jnp.where(below_half, s_ref[a, :half, :half], -jnp.inf)
        cross = s_ref[a, :half, half:]
        late = jnp.where(below_half, s_ref[a, half:, half:], -jnp.inf)
        m_early = jnp.max(early, axis=0, keepdims=True)
        m_late = jnp.maximum(jnp.max(cross, axis=0, keepdims=True), jnp.max(late, axis=0, keepdims=True))
        p_top = jnp.concatenate([jnp.exp2(early - m_early), jnp.exp2(cross - m_late)], axis=1).astype(BF16)
        p_late = jnp.exp2(late - m_late).astype(BF16)
        top = jnp.dot(v_ref[0, a, :, pl.ds(ds, half)], p_top, preferred_element_type=F32)
        bottom = jnp.dot(v_ref[0, a, :, pl.ds(ds + half, half)], p_late, preferred_element_type=F32)
        acc_ref[d, a, :, :half] = top[:, :half]
        acc_ref[d, a, :, half:] = top[:, half:] + bottom
        m_ref[d, a] = jnp.concatenate([m_early, m_late], axis=1)

    def softmax_pv(a, qi, kj, s_ref):
        ks = pl.multiple_of(kj * tile, tile)
        vt = v_ref[0, a, :, pl.ds(ks, tile)]
        s = s_ref[a]
        m_prev = m_ref[qi, a]
        m_new = jnp.maximum(m_prev, jnp.max(s, axis=0, keepdims=True))
        p = jnp.exp2(s - m_new).astype(BF16)
        acc_ref[qi, a] = jnp.exp2(m_prev - m_new) * acc_ref[qi, a] + jnp.dot(vt, p, preferred_element_type=F32)
        m_ref[qi, a] = m_new

    bufs = (s0_ref, s1_ref)
    everyone = range(per_block)
    m_ref[nq] = jnp.zeros(m_ref.shape[1:], F32)
    acc_ref[nq] = jnp.zeros(acc_ref.shape[1:], F32)

    for a in everyone:
        diagonal_logits(a, 0, s0_ref)

    def diagonal_steps(u, _):
        for r in range(unroll):
            d = unroll * u + r
            nxt = jnp.minimum(d + 1, nq - 1)
            for a in everyone:
                diagonal_logits(a, nxt, bufs[(r + 1) % 2])
            for a in everyone:
                diagonal_softmax_pv(a, d, bufs[r % 2])
        return 0

    lax.fori_loop(0, nq // unroll, diagonal_steps, 0)

    for a in everyone:
        head = pl.program_id(1) * per_block + a
        base = (pl.program_id(0) * heads + head) * tasks
        logits(a, tq_ref[base], tk_ref[base], s0_ref)

        def lower_steps(count, first, a=a, base=base):
            def run(u, _):
                for r in range(count):
                    t = base + first + count * u + r
                    nxt = jnp.minimum(t + 1, base + tasks - 1)
                    logits(a, tq_ref[nxt], tk_ref[nxt], bufs[(r + 1) % 2])
                    softmax_pv(a, tq_ref[t], tk_ref[t], bufs[r % 2])
                return 0
            return run

        kept = count_ref[pl.program_id(0) * heads + head]
        main = kept // unroll
        lax.fori_loop(0, main, lower_steps(unroll, 0), 0)
        lax.fori_loop(0, (kept - main * unroll + 1) // 2, lower_steps(2, main * unroll), 0)

    def finish(qi, _):
        qs = pl.multiple_of(qi * tile, tile)
        o_t = jnp.concatenate([acc_ref[qi, a, :dh, :] / acc_ref[qi, a, dh:dh + 1, :] for a in everyone],
                              axis=0)
        o_ref[pl.ds(qs, tile), :] = (o_t.T * _silu(z_ref[pl.ds(qs, tile), :].astype(F32))).astype(o_ref.dtype)
        return 0

    lax.fori_loop(0, nq, finish, 0)


FOX_UNDERFLOW_LOG2 = 150.0
FOX_NORM_SLACK = 1.02
FOX_BIAS_SLACK = 2.0


def _fox_task_lists(edges, q_gain, k_gain, *, batch, heads, dh, nq):
    qk_bound = FOX_NORM_SLACK * dh ** 0.5 * LOG2E * jnp.max(jnp.abs(q_gain)) * jnp.max(jnp.abs(k_gain))
    threshold = 2.0 * qk_bound + FOX_UNDERFLOW_LOG2 + FOX_BIAS_SLACK
    edges = edges.reshape(batch, nq, 8, LANES)
    first = edges[:, :, 0, :heads].transpose(0, 2, 1)
    last = edges[:, :, 1, :heads].transpose(0, 2, 1)
    qi = jnp.arange(nq)[:, None]
    kj = jnp.arange(nq)[None, :]
    keep = (kj < qi) & (first[..., :, None] - last[..., None, :] >= -threshold)
    start = jnp.min(jnp.where(keep, kj, qi), axis=-1)
    length = jnp.arange(nq) - start
    ends = jnp.cumsum(length, axis=-1)
    count = ends[..., -1]
    tasks = nq * (nq - 1) // 2
    t = jnp.arange(tasks)
    row = jnp.sum(ends[..., None, :] <= t[:, None], axis=-1)
    in_row = row[..., None] == jnp.arange(nq)
    tk = t + jnp.sum(jnp.where(in_row, (start - ends + length)[..., None, :], 0), axis=-1)
    tq = jnp.where(t < count[..., None], row, nq).astype(jnp.int32)
    tk = jnp.where(t < count[..., None], tk, 0).astype(jnp.int32)
    return tq.reshape(-1), tk.reshape(-1), count.astype(jnp.int32).reshape(-1), tasks


def _fox_attn(qa, ka, vt, h_main, edges, q_gain, k_gain, *, batch, seq, heads, dh, tile, unroll):
    m = qa.shape[0]
    nq = seq // tile
    assert unroll % 2 == 0 and nq % unroll == 0 and (nq * (nq - 1) // 2) % 2 == 0
    per_block = LANES // dh
    blocks = heads // per_block
    z_off = 3 * blocks
    wide = per_block * LANES
    tq, tk, count, tasks = _fox_task_lists(edges, q_gain, k_gain, batch=batch, heads=heads, dh=dh, nq=nq)
    grid_spec = pltpu.PrefetchScalarGridSpec(
        num_scalar_prefetch=3,
        grid=(batch, blocks),
        in_specs=[pl.BlockSpec((seq, wide), lambda b, h, *_: (b, h)),
                  pl.BlockSpec((seq, wide), lambda b, h, *_: (b, h)),
                  pl.BlockSpec((1, per_block, FOX_VROWS, seq), lambda b, h, *_: (b, h, 0, 0)),
                  pl.BlockSpec((seq, LANES), lambda b, h, *_: (b, z_off + h))],
        out_specs=pl.BlockSpec((seq, LANES), lambda b, h, *_: (b, h)),
        scratch_shapes=[pltpu.VMEM((per_block, tile, tile), F32), pltpu.VMEM((per_block, tile, tile), F32),
                        pltpu.VMEM((nq + 1, per_block, 1, tile), F32),
                        pltpu.VMEM((nq + 1, per_block, FOX_VROWS, tile), F32)])
    return pl.pallas_call(
        functools.partial(_fox_attn_kernel, tile=tile, dh=dh, per_block=per_block, nq=nq, unroll=unroll,
                          heads=heads, tasks=tasks),
        grid_spec=grid_spec,
        out_shape=jax.ShapeDtypeStruct((m, heads * dh), BF16),
        compiler_params=_params("parallel", "parallel"),
        name="fox_attn",
    )(tq, tk, count, qa, ka, vt, h_main)


def _post_kernel(a_ref, x_ref, p_ref, wo_ref, g_ref, b_ref, wg_ref, wp_ref, o_ref, *, alpha, parts):
    sub = a_ref.shape[0] // parts
    blocks = [slice(r * sub, (r + 1) * sub) for r in range(parts)]
    proj = [jnp.dot(a_ref[rs, :], wo_ref[...], preferred_element_type=F32) for rs in blocks]
    emb = [_mm(p_ref[rs, :], wp_ref[...]) for rs in blocks]
    normed = []
    for rs, y in zip(blocks, proj):
        t = alpha * x_ref[rs, :] + y
        mu = jnp.mean(t, axis=-1, keepdims=True)
        d = t - mu
        var = jnp.mean(d * d, axis=-1, keepdims=True)
        normed.append(d * lax.rsqrt(var + LN_EPS) * g_ref[...] + b_ref[...])
    gates = [_mm(xn, wg_ref[...]) for xn in normed]
    for rs, xn, gate, e in zip(blocks, normed, gates, emb):
        o_ref[rs, :] = xn + _sigmoid(gate) * e


def _post(a, x, p, w_out, ln_g, ln_b, w_gate, w_proj, *, layer, mixer, alpha, tm):
    m, d = x.shape
    dp = p.shape[2]

    def rows(n):
        return pl.BlockSpec((tm, n), lambda i: (i, 0))

    def of_layer(index, r, n):
        return pl.BlockSpec((None, r, n), lambda i: (index, 0, 0))

    return pl.pallas_call(
        functools.partial(_post_kernel, alpha=alpha, parts=2),
        grid=(m // tm,),
        in_specs=[rows(a.shape[1]), rows(d), pl.BlockSpec((None, tm, dp), lambda i: (layer, i, 0)),
                  of_layer(mixer, a.shape[1], d), of_layer(layer, 1, d), of_layer(layer, 1, d),
                  of_layer(layer, d, d), of_layer(layer, dp, d)],
        out_specs=rows(d),
        out_shape=jax.ShapeDtypeStruct((m, d), F32),
        compiler_params=_params("parallel"),
        name="post_block",
    )(a, x, p, w_out, ln_g, ln_b, w_gate, w_proj)


def _pad_cols(w, n):
    return jnp.pad(w, ((0, 0), (0, n - w.shape[1])))


def _pad_row(v, n):
    return jnp.pad(v, (0, n - v.shape[0])).reshape(1, n)


def _gdn_layer(x, w_stack, layer, w_in, conv_w, a_log, dt_bias, norm_g, *, batch, seq):
    heads = a_log.shape[0]
    dk = norm_g.shape[0]
    wide = 4 * heads * dk
    w_gates = jnp.concatenate([_pad_cols(w_in[:, wide:wide + heads], LANES),
                               _pad_cols(w_in[:, wide + heads:], LANES)], axis=1).astype(BF16)
    h_main, hs = _in_proj(x, w_stack, layer, wide, w_gates, tm=1024, tn=1024, name="gdn_in_proj")
    beta, gcum, gcum_t = _gdn_gates(hs, _pad_row(a_log, LANES), _pad_row(dt_bias, LANES), rows=2048)
    return _gdn_core(h_main, conv_w, beta, gcum, gcum_t, norm_g.reshape(1, dk),
                     batch=batch, seq=seq, heads=heads, dk=dk, rows=512, hb=8)


def _fox_layer(x, w_stack, layer, w_in, b_f, q_norm_g, k_norm_g, *, batch, seq):
    heads = b_f.shape[0]
    dh = q_norm_g.shape[0]
    wide = 4 * heads * dh
    h_main, hf = _in_proj(x, w_stack, layer, wide, _pad_cols(w_in[:, wide:], LANES).astype(BF16),
                          tm=1024, tn=1024, name="fox_in_proj")
    qa, ka, vt, edges = _fox_prep(h_main, hf, _pad_row(b_f, LANES),
                                  jnp.tile(q_norm_g, heads).reshape(1, heads * dh),
                                  jnp.tile(k_norm_g, heads).reshape(1, heads * dh),
                                  batch=batch, seq=seq, heads=heads, dh=dh, rows=FOX_TILE)
    return _fox_attn(qa, ka, vt, h_main, edges, q_norm_g, k_norm_g, batch=batch, seq=seq, heads=heads, dh=dh,
                     tile=FOX_TILE, unroll=8)


def kernel(x, p, ln_g, ln_b, ple_w_gate, ple_w_proj, gdn_w_in, gdn_conv_w, gdn_a_log, gdn_dt_bias,
           gdn_norm_g, gdn_w_out, fox_w_in, fox_b_f, fox_q_norm_g, fox_k_norm_g, fox_w_out):
    batch, seq, d = x.shape
    depth = ln_g.shape[0]
    alpha = (2 * depth) ** 0.25
    xf = x.reshape(batch * seq, d)
    p_rows = p.reshape(depth, batch * seq, -1)
    ln_g3, ln_b3 = ln_g.reshape(depth, 1, d), ln_b.reshape(depth, 1, d)
    w_gate, w_proj = ple_w_gate.astype(BF16), ple_w_proj.astype(BF16)
    gdn_w, fox_w = gdn_w_in.astype(BF16), fox_w_in.astype(BF16)
    gdn_wo, fox_wo = gdn_w_out.astype(BF16), fox_w_out.astype(BF16)
    for i in range(depth):
        j = i // 2
        if i % 2 == 0:
            a = _gdn_layer(xf, gdn_w, j, gdn_w_in[j], gdn_conv_w[j], gdn_a_log[j], gdn_dt_bias[j], gdn_norm_g[j],
                           batch=batch, seq=seq)
            w_out = gdn_wo
        else:
            a = _fox_layer(xf, fox_w, j, fox_w_in[j], fox_b_f[j], fox_q_norm_g[j], fox_k_norm_g[j],
                           batch=batch, seq=seq)
            w_out = fox_wo
        xf = _post(a, xf, p_rows, w_out, ln_g3, ln_b3, w_gate, w_proj, layer=i, mixer=j, alpha=alpha, tm=1024)
    return xf.reshape(batch, seq, d)
```

```python
import functools

import jax
import jax.numpy as jnp
import numpy as np
from jax import lax
from jax.experimental import pallas as pl
from jax.experimental.pallas import tpu as pltpu

F32 = jnp.float32
BF16 = jnp.bfloat16

LANES = 128
V7X_VMEM_BYTES = 64 * 1024 * 1024
VMEM_LIMIT = V7X_VMEM_BYTES * 3 // 4

GDN_CHUNK = 64
LN_EPS = 1e-5
RMS_EPS = 1e-6
LOG2E = 1.4426950408889634

_NT = (((1,), (1,)), ((), ()))
_TN = (((0,), (0,)), ((), ()))


def _mm(a, b):
    return jnp.dot(a.astype(BF16), b.astype(BF16), preferred_element_type=F32)


def _sigmoid(x):
    return 1.0 / (1.0 + jnp.exp(-x))


def _silu(x):
    h = 0.5 * x
    return h + h * jnp.tanh(h)


def _softplus(x):
    return jnp.maximum(x, 0.0) + jnp.log1p(jnp.exp(-jnp.abs(x)))


def _split3(x):
    hi = x.astype(BF16).astype(F32)
    r = x - hi
    mid = r.astype(BF16).astype(F32)
    lo = (r - mid).astype(BF16).astype(F32)
    return hi, mid, lo


def _iota2(shape, dim):
    return lax.broadcasted_iota(jnp.int32, shape, dim)


def _params(*semantics):
    return pltpu.CompilerParams(dimension_semantics=semantics, vmem_limit_bytes=VMEM_LIMIT)


def _in_proj_kernel(x_ref, wm_ref, ws_ref, hm_ref, hs_ref, *, tn):
    xb = x_ref[...].astype(BF16)
    for j in range(wm_ref.shape[1] // tn):
        cols = slice(j * tn, (j + 1) * tn)
        hm_ref[:, cols] = jnp.dot(xb, wm_ref[:, cols], preferred_element_type=F32).astype(hm_ref.dtype)
    hs_ref[...] = jnp.dot(xb, ws_ref[...], preferred_element_type=F32)


def _in_proj(x, w_stack, layer, n, w_small, *, tm, tn, name):
    m, k = x.shape
    ns = w_small.shape[1]
    return pl.pallas_call(
        functools.partial(_in_proj_kernel, tn=tn),
        grid=(m // tm,),
        in_specs=[pl.BlockSpec((tm, k), lambda i: (i, 0)),
                  pl.BlockSpec((None, k, n), lambda i: (layer, 0, 0)),
                  pl.BlockSpec((k, ns), lambda i: (0, 0))],
        out_specs=[pl.BlockSpec((tm, n), lambda i: (i, 0)),
                   pl.BlockSpec((tm, ns), lambda i: (i, 0))],
        out_shape=[jax.ShapeDtypeStruct((m, n), BF16), jax.ShapeDtypeStruct((m, ns), F32)],
        compiler_params=_params("parallel"),
        name=name,
    )(x, w_stack, w_small)


def _gdn_gate_kernel(hs_ref, alog_ref, dtb_ref, beta_ref, gcum_ref, gcum_t_ref, *, rows):
    c = GDN_CHUNK
    beta_ref[...] = _sigmoid(hs_ref[:, :LANES])
    g = -jnp.exp(alog_ref[...]) * _softplus(hs_ref[:, LANES:] + dtb_ref[...])
    g3 = jnp.concatenate([part.astype(BF16) for part in _split3(g)], axis=1)
    tril = (_iota2((c, c), 0) >= _iota2((c, c), 1)).astype(BF16)
    eye = (_iota2((8, LANES), 0) == _iota2((8, LANES), 1)).astype(BF16)
    for i in range(rows // c):
        sums = jnp.dot(tril, g3[i * c:(i + 1) * c], preferred_element_type=F32)
        gc = sums[:, :LANES] + sums[:, LANES:2 * LANES] + sums[:, 2 * LANES:]
        gcum_ref[i * c:(i + 1) * c, :] = gc
        hi, mid, lo = (lax.dot_general(eye, part.astype(BF16), _NT, preferred_element_type=F32)
                       for part in _split3(gc))
        gcum_t_ref[i] = hi + mid + lo


def _gdn_gates(hs, alog_row, dtb_row, *, rows):
    m = hs.shape[0]
    c = GDN_CHUNK
    return pl.pallas_call(
        functools.partial(_gdn_gate_kernel, rows=rows),
        grid=(m // rows,),
        in_specs=[pl.BlockSpec((rows, 2 * LANES), lambda i: (i, 0)),
                  pl.BlockSpec((1, LANES), lambda i: (0, 0)),
                  pl.BlockSpec((1, LANES), lambda i: (0, 0))],
        out_specs=[pl.BlockSpec((rows, LANES), lambda i: (i, 0)),
                   pl.BlockSpec((rows, LANES), lambda i: (i, 0)),
                   pl.BlockSpec((rows // c, 8, c), lambda i: (i, 0, 0))],
        out_shape=[jax.ShapeDtypeStruct((m, LANES), F32),
                   jax.ShapeDtypeStruct((m, LANES), F32),
                   jax.ShapeDtypeStruct((m // c, 8, c), F32)],
        compiler_params=_params("parallel"),
        name="gdn_gates",
    )(hs, alog_row, dtb_row)


def _gdn_core_kernel(hq_ref, hk_ref, hv_ref, hz_ref, wq_ref, wk_ref, wv_ref, beta_ref, gcum_ref,
                     gcum_t_ref, ng_ref, o_ref, state_ref, xe_ref, *, rows, dk, hb):
    c = GDN_CHUNK
    n = rows // c
    group = pl.program_id(1)

    @pl.when(pl.program_id(2) == 0)
    def _():
        state_ref[...] = jnp.zeros_like(state_ref)
        xe_ref[:, :8, :] = jnp.zeros((3, 8, xe_ref.shape[2]), F32)

    def conv_silu(x_ref, w_ref, slot):
        w = w_ref[...]
        taps = w.shape[0]
        xe_ref[slot, 8:, :] = x_ref[...].astype(F32)
        y = w[taps - 1:taps, :] * xe_ref[slot, 8:, :]
        for t in range(taps - 1):
            y = y + w[t:t + 1, :] * xe_ref[slot, pl.ds(8 - taps + 1 + t, rows), :]
        xe_ref[slot, :8, :] = xe_ref[slot, rows:, :]
        return _silu(y)

    def l2norm(x):
        return x * lax.rsqrt(jnp.sum(x * x, axis=-1, keepdims=True) + RMS_EPS)

    k_all = conv_silu(hk_ref, wk_ref, 1)
    q_all = conv_silu(hq_ref, wq_ref, 0)
    v_all = conv_silu(hv_ref, wv_ref, 2)
    lane = _iota2((rows, LANES), 1)
    ri = _iota2((c, c), 0)
    ci = _iota2((c, c), 1)
    causal = ri >= ci
    strict = ri > ci
    eye = (ri == ci).astype(F32)
    dv = dk

    units = []
    for a in range(hb):
        head = group * hb + a
        cols = slice(a * dk, (a + 1) * dk)
        k = l2norm(k_all[:, cols])
        q = l2norm(q_all[:, cols]) * (dk ** -0.5)
        v = v_all[:, cols]
        beta = jnp.sum(jnp.where(lane == head, beta_ref[...], 0.0), axis=1, keepdims=True)
        gcol = jnp.sum(jnp.where(lane == head, gcum_ref[...], 0.0), axis=1, keepdims=True)
        for i in range(n):
            sl = slice(i * c, (i + 1) * c)
            units.append(dict(a=a, q=q[sl], k=k[sl], v=v[sl], beta=beta[sl], g=gcol[sl],
                              grow=gcum_t_ref[i, pl.ds(head, 1), :]))
    for u in units:
        u["glast"] = u["g"][c - 1:c, :]
        u["eg"] = jnp.exp(u["g"])
        u["decay"] = jnp.where(causal, jnp.exp(u["g"] - u["grow"]), 0.0)
        u["kb"] = u["k"] * u["beta"]
    for u in units:
        kbf = u["k"].astype(BF16)
        u["kk"] = lax.dot_general(u["kb"].astype(BF16), kbf, _NT, preferred_element_type=F32)
        u["qk"] = lax.dot_general(u["q"].astype(BF16), kbf, _NT, preferred_element_type=F32)
    for u in units:
        u["lmat"] = jnp.where(strict, u["kk"] * u["decay"], 0.0)
        u["amat"] = jnp.where(causal, u["qk"] * u["decay"], 0.0)
        u["tinv"] = eye - u["lmat"]
    for u in units:
        u["pk"] = _mm(u["lmat"], u["lmat"])
    span = 4
    while span < c:
        for u in units:
            u["both"] = _mm(u["pk"], jnp.concatenate([u["tinv"], u["pk"]], axis=1))
        for u in units:
            u["tinv"] = u["tinv"] + u["both"][:, :c]
            u["pk"] = u["both"][:, c:]
        span *= 2
    for u in units:
        u["tinv"] = u["tinv"] + _mm(u["pk"], u["tinv"])
    for u in units:
        u["x"] = _mm(u["tinv"], jnp.concatenate([u["v"] * u["beta"], u["kb"] * u["eg"]], axis=1))
    for u in units:
        u["ax"] = _mm(u["amat"], u["x"])
        kd = u["k"] * jnp.exp(u["glast"] - u["g"])
        u["kx"] = lax.dot_general(kd.astype(BF16), u["x"].astype(BF16), _TN, preferred_element_type=F32)
    for u in units:
        u["qeff"] = u["q"] * u["eg"] - u["ax"][:, dv:]
    states = [state_ref[a] for a in range(hb)]
    outs = [[] for _ in range(hb)]
    for i in range(n):
        for a in range(hb):
            u = units[a * n + i]
            sb = states[a].astype(BF16)
            outs[a].append(_mm(u["qeff"], sb) + u["ax"][:, :dv])
            states[a] = jnp.exp(u["glast"]) * states[a] + u["kx"][:, :dv] - _mm(u["kx"][:, dv:], sb)
    for a in range(hb):
        state_ref[a] = states[a]
        o = jnp.concatenate(outs[a], axis=0)
        o = o * lax.rsqrt(jnp.mean(o * o, axis=-1, keepdims=True) + RMS_EPS) * ng_ref[...]
        cols = slice(a * dk, (a + 1) * dk)
        o_ref[:, cols] = (o * _silu(hz_ref[:, cols].astype(F32))).astype(o_ref.dtype)


def _gdn_core(h_main, conv_w, beta, gcum, gcum_t, ng_row, *, batch, seq, heads, dk, rows, hb):
    m = h_main.shape[0]
    nb = seq // rows
    c = GDN_CHUNK
    taps = conv_w.shape[0]
    groups = heads // hb
    wide = hb * dk

    def col(off):
        return pl.BlockSpec((rows, wide), lambda b, h, j: (b * nb + j, off + h))

    def wcol(off):
        return pl.BlockSpec((taps, wide), lambda b, h, j: (0, off + h))

    def gate():
        return pl.BlockSpec((rows, LANES), lambda b, h, j: (b * nb + j, 0))

    return pl.pallas_call(
        functools.partial(_gdn_core_kernel, rows=rows, dk=dk, hb=hb),
        grid=(batch, groups, nb),
        in_specs=[col(0), col(groups), col(2 * groups), col(3 * groups),
                  wcol(0), wcol(groups), wcol(2 * groups),
                  gate(), gate(),
                  pl.BlockSpec((rows // c, 8, c), lambda b, h, j: (b * nb + j, 0, 0)),
                  pl.BlockSpec((1, dk), lambda b, h, j: (0, 0))],
        out_specs=pl.BlockSpec((rows, wide), lambda b, h, j: (b * nb + j, h)),
        out_shape=jax.ShapeDtypeStruct((m, heads * dk), BF16),
        scratch_shapes=[pltpu.VMEM((hb, dk, dk), F32), pltpu.VMEM((3, rows + 8, wide), F32)],
        compiler_params=_params("parallel", "parallel", "arbitrary"),
        name="gdn_core",
    )(h_main, h_main, h_main, h_main, conv_w, conv_w, conv_w, beta, gcum, gcum_t, ng_row)


FOX_TILE = 512
FOX_VROWS = 80


def _fox_prep_kernel(hq_ref, hk_ref, hv_ref, hf_ref, bf_ref, qg_ref, kg_ref, selq_ref, selk_ref,
                     qa_ref, ka_ref, vt_ref, edge_ref, carry_ref, *, rows, heads, dh):
    @pl.when(pl.program_id(1) == 0)
    def _():
        carry_ref[...] = jnp.zeros_like(carry_ref)

    logf = -_softplus(-(hf_ref[...] + bf_ref[...]))
    lane = _iota2((rows, LANES), 1)
    logf = jnp.where(lane < heads, logf, 0.0)
    tril = (_iota2((rows, rows), 0) >= _iota2((rows, rows), 1)).astype(BF16)
    sums = jnp.dot(tril, jnp.concatenate([part.astype(BF16) for part in _split3(logf)], axis=1),
                   preferred_element_type=F32)
    c = sums[:, :LANES] + sums[:, LANES:2 * LANES] + sums[:, 2 * LANES:] + carry_ref[...]
    carry_ref[...] = c[rows - 1:rows, :]
    c2 = c * LOG2E
    edge_ref[0] = jnp.concatenate([c2[:1], c2[rows - 1:], jnp.zeros((6, LANES), F32)], axis=0)
    chi, cmid, clo = _split3(c2)
    parts = (chi + pltpu.roll(cmid, heads, axis=1) + pltpu.roll(clo, 2 * heads, axis=1)
             + jnp.where(lane == 3 * heads, 1.0, 0.0)).astype(BF16)
    q_tails = jnp.dot(parts, selq_ref[...], preferred_element_type=F32)
    k_tails = jnp.dot(parts, selk_ref[...], preferred_element_type=F32)

    same_head = _iota2((LANES, LANES), 0) // dh == _iota2((LANES, LANES), 1) // dh
    head_mean = jnp.where(same_head, 1.0 / dh, 0.0).astype(BF16)

    def head_rms(x, g, scale):
        mean_sq = jnp.dot((x * x).astype(BF16), head_mean, preferred_element_type=F32)
        return x * lax.rsqrt(mean_sq + RMS_EPS) * (g * scale)

    per_block = LANES // dh
    vrow = _iota2((FOX_VROWS, LANES), 0)
    vlane = _iota2((FOX_VROWS, LANES), 1)
    out_row = _iota2((FOX_VROWS, rows), 0)
    for j in range(heads // per_block):
        cols = slice(j * LANES, (j + 1) * LANES)
        qn = head_rms(hq_ref[:, cols].astype(F32), qg_ref[:, cols], dh ** -0.5 * LOG2E)
        kn = head_rms(hk_ref[:, cols].astype(F32), kg_ref[:, cols], 1.0)
        vb = hv_ref[:, cols].astype(BF16)
        for a in range(per_block):
            h = j * per_block + a
            qh = qn if a == 0 else pltpu.roll(qn, LANES - a * dh, axis=1)
            kh = kn if a == 0 else pltpu.roll(kn, LANES - a * dh, axis=1)
            out_cols = slice(h * LANES, (h + 1) * LANES)
            qa_ref[:, out_cols] = jnp.where(lane < dh, qh, q_tails[:, out_cols]).astype(qa_ref.dtype)
            ka_ref[:, out_cols] = jnp.where(lane < dh, kh, k_tails[:, out_cols]).astype(ka_ref.dtype)
            pick = ((vlane == a * dh + vrow) & (vrow < dh)).astype(BF16)
            vt = lax.dot_general(pick, vb, _NT, preferred_element_type=F32)
            vt_ref[0, h] = jnp.where(out_row == dh, 1.0, vt).astype(vt_ref.dtype)


def _fox_bias_selectors(heads, dh):
    j = np.arange(LANES)[:, None]
    col = np.arange(heads * LANES)[None, :]
    h, l = col // LANES, col % LANES - dh
    part = lambda i: (l == i) & (j == i * heads + h)
    ones = lambda lo: (l >= lo) & (l < lo + 3) & (j == 3 * heads)
    sel_q = part(0) | part(1) | part(2) | ones(3)
    sel_k = ones(0).astype(np.float32) - ((l == 3) & (j == h)) - ((l == 4) & (j == heads + h)) \
        - ((l == 5) & (j == 2 * heads + h))
    return jnp.asarray(sel_q, BF16), jnp.asarray(sel_k, BF16)


def _fox_prep(h_main, hf, bf_row, qg_row, kg_row, *, batch, seq, heads, dh, rows):
    m = h_main.shape[0]
    w = heads * dh
    nt = seq // rows
    sel_q, sel_k = _fox_bias_selectors(heads, dh)

    def col(j):
        return pl.BlockSpec((rows, w), lambda b, t: (b * nt + t, j))

    def const(r, n):
        return pl.BlockSpec((r, n), lambda b, t: (0, 0))

    aug = pl.BlockSpec((rows, heads * LANES), lambda b, t: (b * nt + t, 0))
    return pl.pallas_call(
        functools.partial(_fox_prep_kernel, rows=rows, heads=heads, dh=dh),
        grid=(batch, nt),
        in_specs=[col(0), col(1), col(2),
                  pl.BlockSpec((rows, LANES), lambda b, t: (b * nt + t, 0)),
                  const(1, LANES), const(1, w), const(1, w),
                  const(LANES, heads * LANES), const(LANES, heads * LANES)],
        out_specs=[aug, aug,
                   pl.BlockSpec((1, heads, FOX_VROWS, rows), lambda b, t: (b, 0, 0, t)),
                   pl.BlockSpec((1, 8, LANES), lambda b, t: (b * nt + t, 0, 0))],
        out_shape=[jax.ShapeDtypeStruct((m, heads * LANES), BF16),
                   jax.ShapeDtypeStruct((m, heads * LANES), BF16),
                   jax.ShapeDtypeStruct((batch, heads, FOX_VROWS, seq), BF16),
                   jax.ShapeDtypeStruct((batch * nt, 8, LANES), F32)],
        scratch_shapes=[pltpu.VMEM((1, LANES), F32)],
        compiler_params=_params("parallel", "arbitrary"),
        name="fox_prep",
    )(h_main, h_main, h_main, hf, bf_row, qg_row, kg_row, sel_q, sel_k)


def _fox_attn_kernel(tq_ref, tk_ref, count_ref, q_ref, k_ref, v_ref, z_ref, o_ref, s0_ref, s1_ref, m_ref, acc_ref,
                     *, tile, dh, per_block, nq, unroll, heads, tasks):
    half = tile // 2
    below_half = _iota2((half, half), 0) <= _iota2((half, half), 1)

    def logits(a, qi, kj, s_ref):
        qs = pl.multiple_of(jnp.minimum(qi, nq - 1) * tile, tile)
        ks = pl.multiple_of(kj * tile, tile)
        s_ref[a] = lax.dot_general(k_ref[pl.ds(ks, tile), a * LANES:(a + 1) * LANES],
                                   q_ref[pl.ds(qs, tile), a * LANES:(a + 1) * LANES], _NT,
                                   preferred_element_type=F32)

    def diagonal_logits(a, d, s_ref):
        ds = pl.multiple_of(d * tile, tile)
        lanes = slice(a * LANES, (a + 1) * LANES)
        s_ref[a, :half, :] = lax.dot_general(k_ref[pl.ds(ds, half), lanes], q_ref[pl.ds(ds, tile), lanes], _NT,
                                             preferred_element_type=F32)
        s_ref[a, half:, half:] = lax.dot_general(k_ref[pl.ds(ds + half, half), lanes],
                                                 q_ref[pl.ds(ds + half, half), lanes], _NT,
                                                 preferred_element_type=F32)

    def diagonal_softmax_pv(a, d, s_ref):
        ds = pl.multiple_of(d * tile, tile)
        early = jnp.where(below_half, s_ref[a, :half, :half], -jnp.inf)
        cross = s_ref[a, :half, half:]
        late = jnp.where(below_half, s_ref[a, half:, half:], -jnp.inf)
        m_early = jnp.max(early, axis=0, keepdims=True)
        m_late = jnp.maximum(jnp.max(cross, axis=0, keepdims=True), jnp.max(late, axis=0, keepdims=True))
        p_top = jnp.concatenate([jnp.exp2(early - m_early), jnp.exp2(cross - m_late)], axis=1).astype(BF16)
        p_late = jnp.exp2(late - m_late).astype(BF16)
        top = jnp.dot(v_ref[0, a, :, pl.ds(ds, half)], p_top, preferred_element_type=F32)
        bottom = jnp.dot(v_ref[0, a, :, pl.ds(ds + half, half)], p_late, preferred_element_type=F32)
        acc_ref[d, a, :, :half] = top[:, :half]
        acc_ref[d, a, :, half:] = top[:, half:] + bottom
        m_ref[d, a] = jnp.concatenate([m_early, m_late], axis=1)

    def softmax_pv(a, qi, kj, s_ref):
        ks = pl.multiple_of(kj * tile, tile)
        vt = v_ref[0, a, :, pl.ds(ks, tile)]
        s = s_ref[a]
        m_prev = m_ref[qi, a]
        m_new = jnp.maximum(m_prev, jnp.max(s, axis=0, keepdims=True))
        p = jnp.exp2(s - m_new).astype(BF16)
        acc_ref[qi, a] = jnp.exp2(m_prev - m_new) * acc_ref[qi, a] + jnp.dot(vt, p, preferred_element_type=F32)
        m_ref[qi, a] = m_new

    bufs = (s0_ref, s1_ref)
    everyone = range(per_block)
    m_ref[nq] = jnp.zeros(m_ref.shape[1:], F32)
    acc_ref[nq] = jnp.zeros(acc_ref.shape[1:], F32)

    for a in everyone:
        diagonal_logits(a, 0, s0_ref)

    def diagonal_steps(u, _):
        for r in range(unroll):
            d = unroll * u + r
            nxt = jnp.minimum(d + 1, nq - 1)
            for a in everyone:
                diagonal_logits(a, nxt, bufs[(r + 1) % 2])
            for a in everyone:
                diagonal_softmax_pv(a, d, bufs[r % 2])
        return 0

    lax.fori_loop(0, nq // unroll, diagonal_steps, 0)

    for a in everyone:
        head = pl.program_id(1) * per_block + a
        base = (pl.program_id(0) * heads + head) * tasks
        logits(a, tq_ref[base], tk_ref[base], s0_ref)

        def lower_steps(count, first, a=a, base=base):
            def run(u, _):
                for r in range(count):
                    t = base + first + count * u + r
                    nxt = jnp.minimum(t + 1, base + tasks - 1)
                    logits(a, tq_ref[nxt], tk_ref[nxt], bufs[(r + 1) % 2])
                    softmax_pv(a, tq_ref[t], tk_ref[t], bufs[r % 2])
                return 0
            return run

        kept = count_ref[pl.program_id(0) * heads + head]
        main = kept // unroll
        lax.fori_loop(0, main, lower_steps(unroll, 0), 0)
        lax.fori_loop(0, (kept - main * unroll + 1) // 2, lower_steps(2, main * unroll), 0)

    def finish(qi, _):
        qs = pl.multiple_of(qi * tile, tile)
        o_t = jnp.concatenate([acc_ref[qi, a, :dh, :] / acc_ref[qi, a, dh:dh + 1, :] for a in everyone],
                              axis=0)
        o_ref[pl.ds(qs, tile), :] = (o_t.T * _silu(z_ref[pl.ds(qs, tile), :].astype(F32))).astype(o_ref.dtype)
        return 0

    lax.fori_loop(0, nq, finish, 0)


FOX_UNDERFLOW_LOG2 = 150.0
FOX_NORM_SLACK = 1.02
FOX_BIAS_SLACK = 2.0


def _fox_task_lists(edges, q_gain, k_gain, *, batch, heads, dh, nq):
    qk_bound = FOX_NORM_SLACK * dh ** 0.5 * LOG2E * jnp.max(jnp.abs(q_gain)) * jnp.max(jnp.abs(k_gain))
    threshold = 2.0 * qk_bound + FOX_UNDERFLOW_LOG2 + FOX_BIAS_SLACK
    edges = edges.reshape(batch, nq, 8, LANES)
    first = edges[:, :, 0, :heads].transpose(0, 2, 1)
    last = edges[:, :, 1, :heads].transpose(0, 2, 1)
    qi = jnp.arange(nq)[:, None]
    kj = jnp.arange(nq)[None, :]
    keep = (kj < qi) & (first[..., :, None] - last[..., None, :] >= -threshold)
    start = jnp.min(jnp.where(keep, kj, qi), axis=-1)
    length = jnp.arange(nq) - start
    ends = jnp.cumsum(length, axis=-1)
    count = ends[..., -1]
    tasks = nq * (nq - 1) // 2
    t = jnp.arange(tasks)
    row = jnp.sum(ends[..., None, :] <= t[:, None], axis=-1)
    in_row = row[..., None] == jnp.arange(nq)
    tk = t + jnp.sum(jnp.where(in_row, (start - ends + length)[..., None, :], 0), axis=-1)
    tq = jnp.where(t < count[..., None], row, nq).astype(jnp.int32)
    tk = jnp.where(t < count[..., None], tk, 0).astype(jnp.int32)
    return tq.reshape(-1), tk.reshape(-1), count.astype(jnp.int32).reshape(-1), tasks


def _fox_attn(qa, ka, vt, h_main, edges, q_gain, k_gain, *, batch, seq, heads, dh, tile, unroll):
    m = qa.shape[0]
    nq = seq // tile
    assert unroll % 2 == 0 and nq % unroll == 0 and (nq * (nq - 1) // 2) % 2 == 0
    per_block = LANES // dh
    blocks = heads // per_block
    z_off = 3 * blocks
    wide = per_block * LANES
    tq, tk, count, tasks = _fox_task_lists(edges, q_gain, k_gain, batch=batch, heads=heads, dh=dh, nq=nq)
    grid_spec = pltpu.PrefetchScalarGridSpec(
        num_scalar_prefetch=3,
        grid=(batch, blocks),
        in_specs=[pl.BlockSpec((seq, wide), lambda b, h, *_: (b, h)),
                  pl.BlockSpec((seq, wide), lambda b, h, *_: (b, h)),
                  pl.BlockSpec((1, per_block, FOX_VROWS, seq), lambda b, h, *_: (b, h, 0, 0)),
                  pl.BlockSpec((seq, LANES), lambda b, h, *_: (b, z_off + h))],
        out_specs=pl.BlockSpec((seq, LANES), lambda b, h, *_: (b, h)),
        scratch_shapes=[pltpu.VMEM((per_block, tile, tile), F32), pltpu.VMEM((per_block, tile, tile), F32),
                        pltpu.VMEM((nq + 1, per_block, 1, tile), F32),
                        pltpu.VMEM((nq + 1, per_block, FOX_VROWS, tile), F32)])
    return pl.pallas_call(
        functools.partial(_fox_attn_kernel, tile=tile, dh=dh, per_block=per_block, nq=nq, unroll=unroll,
                          heads=heads, tasks=tasks),
        grid_spec=grid_spec,
        out_shape=jax.ShapeDtypeStruct((m, heads * dh), BF16),
        compiler_params=_params("parallel", "parallel"),
        name="fox_attn",
    )(tq, tk, count, qa, ka, vt, h_main)


def _post_kernel(a_ref, x_ref, p_ref, wo_ref, g_ref, b_ref, wg_ref, wp_ref, o_ref, *, alpha, parts):
    sub = a_ref.shape[0] // parts
    blocks = [slice(r * sub, (r + 1) * sub) for r in range(parts)]
    proj = [jnp.dot(a_ref[rs, :], wo_ref[...], preferred_element_type=F32) for rs in blocks]
    emb = [_mm(p_ref[rs, :], wp_ref[...]) for rs in blocks]
    normed = []
    for rs, y in zip(blocks, proj):
        t = alpha * x_ref[rs, :] + y
        mu = jnp.mean(t, axis=-1, keepdims=True)
        d = t - mu
        var = jnp.mean(d * d, axis=-1, keepdims=True)
        normed.append(d * lax.rsqrt(var + LN_EPS) * g_ref[...] + b_ref[...])
    gates = [_mm(xn, wg_ref[...]) for xn in normed]
    for rs, xn, gate, e in zip(blocks, normed, gates, emb):
        o_ref[rs, :] = xn + _sigmoid(gate) * e


def _post(a, x, p, w_out, ln_g, ln_b, w_gate, w_proj, *, layer, mixer, alpha, tm):
    m, d = x.shape
    dp = p.shape[2]

    def rows(n):
        return pl.BlockSpec((tm, n), lambda i: (i, 0))

    def of_layer(index, r, n):
        return pl.BlockSpec((None, r, n), lambda i: (index, 0, 0))

    return pl.pallas_call(
        functools.partial(_post_kernel, alpha=alpha, parts=4),
        grid=(m // tm,),
        in_specs=[rows(a.shape[1]), rows(d), pl.BlockSpec((None, tm, dp), lambda i: (layer, i, 0)),
                  of_layer(mixer, a.shape[1], d), of_layer(layer, 1, d), of_layer(layer, 1, d),
                  of_layer(layer, d, d), of_layer(layer, dp, d)],
        out_specs=rows(d),
        out_shape=jax.ShapeDtypeStruct((m, d), F32),
        compiler_params=_params("parallel"),
        name="post_block",
    )(a, x, p, w_out, ln_g, ln_b, w_gate, w_proj)


def _pad_cols(w, n):
    return jnp.pad(w, ((0, 0), (0, n - w.shape[1])))


def _pad_row(v, n):
    return jnp.pad(v, (0, n - v.shape[0])).reshape(1, n)


def _gdn_layer(x, w_stack, layer, w_in, conv_w, a_log, dt_bias, norm_g, *, batch, seq):
    heads = a_log.shape[0]
    dk = norm_g.shape[0]
    wide = 4 * heads * dk
    w_gates = jnp.concatenate([_pad_cols(w_in[:, wide:wide + heads], LANES),
                               _pad_cols(w_in[:, wide + heads:], LANES)], axis=1).astype(BF16)
    h_main, hs = _in_proj(x, w_stack, layer, wide, w_gates, tm=1024, tn=1024, name="gdn_in_proj")
    beta, gcum, gcum_t = _gdn_gates(hs, _pad_row(a_log, LANES), _pad_row(dt_bias, LANES), rows=2048)
    return _gdn_core(h_main, conv_w, beta, gcum, gcum_t, norm_g.reshape(1, dk),
                     batch=batch, seq=seq, heads=heads, dk=dk, rows=512, hb=8)


def _fox_layer(x, w_stack, layer, w_in, b_f, q_norm_g, k_norm_g, *, batch, seq):
    heads = b_f.shape[0]
    dh = q_norm_g.shape[0]
    wide = 4 * heads * dh
    h_main, hf = _in_proj(x, w_stack, layer, wide, _pad_cols(w_in[:, wide:], LANES).astype(BF16),
                          tm=1024, tn=1024, name="fox_in_proj")
    qa, ka, vt, edges = _fox_prep(h_main, hf, _pad_row(b_f, LANES),
                                  jnp.tile(q_norm_g, heads).reshape(1, heads * dh),
                                  jnp.tile(k_norm_g, heads).reshape(1, heads * dh),
                                  batch=batch, seq=seq, heads=heads, dh=dh, rows=FOX_TILE)
    return _fox_attn(qa, ka, vt, h_main, edges, q_norm_g, k_norm_g, batch=batch, seq=seq, heads=heads, dh=dh,
                     tile=FOX_TILE, unroll=8)


def kernel(x, p, ln_g, ln_b, ple_w_gate, ple_w_proj, gdn_w_in, gdn_conv_w, gdn_a_log, gdn_dt_bias,
           gdn_norm_g, gdn_w_out, fox_w_in, fox_b_f, fox_q_norm_g, fox_k_norm_g, fox_w_out):
    batch, seq, d = x.shape
    depth = ln_g.shape[0]
    alpha = (2 * depth) ** 0.25
    xf = x.reshape(batch * seq, d)
    p_rows = p.reshape(depth, batch * seq, -1)
    ln_g3, ln_b3 = ln_g.reshape(depth, 1, d), ln_b.reshape(depth, 1, d)
    w_gate, w_proj = ple_w_gate.astype(BF16), ple_w_proj.astype(BF16)
    gdn_w, fox_w = gdn_w_in.astype(BF16), fox_w_in.astype(BF16)
    gdn_wo, fox_wo = gdn_w_out.astype(BF16), fox_w_out.astype(BF16)
    for i in range(depth):
        j = i // 2
        if i % 2 == 0:
            a = _gdn_layer(xf, gdn_w, j, gdn_w_in[j], gdn_conv_w[j], gdn_a_log[j], gdn_dt_bias[j], gdn_norm_g[j],
                           batch=batch, seq=seq)
            w_out = gdn_wo
        else:
            a = _fox_layer(xf, fox_w, j, fox_w_in[j], fox_b_f[j], fox_q_norm_g[j], fox_k_norm_g[j],
                           batch=batch, seq=seq)
            w_out = fox_wo
        xf = _post(a, xf, p_rows, w_out, ln_g3, ln_b3, w_gate, w_proj, layer=i, mixer=j, alpha=alpha, tm=1024)
    return xf.reshape(batch, seq, d)
```
